```python
import math
import jax
import jax.numpy as jnp
from jax import lax
import numpy as np

D_MODEL = 1024
BATCH = 4
SEQ = 4096
DEPTH = 2

GRID_W = 64
CTX_LEN = 256
EPS = 1e-6
D_MIX = D_MODEL
D_FF = 4 * D_MODEL
N_MOD = 6

S5_WIDTH = D_MIX // 4
S5_GROUP = 16
S5_GROUPS = S5_WIDTH // S5_GROUP
S5_STATE = 64

GDN_HEAD = 64
GDN_HEADS = (D_MIX - S5_WIDTH) // (2 * GDN_HEAD)
GDN_WIDTH = GDN_HEADS * GDN_HEAD
GDN_CONV = 5
GDN_CHUNK = 64

RWKV_HEAD = 64
RWKV_WIDTH = D_MIX - S5_WIDTH - GDN_WIDTH
RWKV_HEADS = RWKV_WIDTH // RWKV_HEAD
DECAY_LORA = 64
ICLR_LORA = 64
GATE_LORA = 128
RWKV_GN_EPS = 64e-5

S5_COLS = S5_WIDTH
GDN_COLS = 4 * GDN_WIDTH + 4 * GDN_HEADS
RWKV_COLS = 3 * RWKV_WIDTH + 2 * DECAY_LORA + 2 * ICLR_LORA + GATE_LORA
IN_COLS = S5_COLS + GDN_COLS + RWKV_COLS

kernel_name = 'hybrid_s5_gdn_rwkv7_prefix_dit'

f32 = jnp.float32


def to_heads(t, n):
    return t.reshape(t.shape[:-1] + (n, t.shape[-1] // n))


def rmsnorm(t, gain):
    t = t.astype(f32)
    return t * lax.rsqrt(jnp.mean(t * t, axis=-1, keepdims=True) + EPS) * gain.astype(f32)


def l2norm(t):
    return t * lax.rsqrt(jnp.sum(t * t, axis=-1, keepdims=True) + EPS)


def sq_relu_mlp(h, w1, w2):
    return jnp.square(jax.nn.relu(h @ w1)) @ w2


def centred_dwconv(t, w):
    pad = w.shape[0] // 2
    return lax.conv_general_dilated(t, w[:, None, :].astype(t.dtype), window_strides=(1,), padding=[(pad, pad)], dimension_numbers=('NWC', 'WIO', 'NWC'), feature_group_count=t.shape[-1])


def grid_qshift(t):
    B, L, C = t.shape
    rows = L // GRID_W
    gp = jnp.pad(t.reshape(B, rows, GRID_W, C), ((0, 0), (1, 1), (1, 1), (0, 0)))
    left, right = gp[:, 1:-1, :-2], gp[:, 1:-1, 2:]
    up, down = gp[:, :-2, 1:-1], gp[:, 2:, 1:-1]
    d = jnp.arange(C) % 4
    out = jnp.where(d == 0, left, jnp.where(d == 1, right, jnp.where(d == 2, up, down)))
    return out.reshape(B, L, C)


def seq_shift(t):
    tp = jnp.pad(t, ((0, 0), (1, 1), (0, 0)))
    even = jnp.arange(t.shape[-1]) % 2 == 0
    return jnp.where(even, tp[:, :-2], tp[:, 2:])


def _linear_op(e1, e2):
    a1, b1 = e1
    a2, b2 = e2
    return a2 * a1, a2 * b1 + b2


def s5_discretise(a_re, a_im, log_dt):
    A = lax.complex(a_re.astype(f32), a_im.astype(f32))
    dt = jnp.exp(log_dt.astype(f32))[:, None]
    a_bar = jnp.exp(A * dt)
    return a_bar, (a_bar - 1.0) / A


def s5_scan(a_bar, bu, h0, reverse):
    if h0 is not None:
        edge = -1 if reverse else 0
        bu = bu.at[:, edge].add(a_bar * h0)
    a = jnp.broadcast_to(a_bar, bu.shape)
    _, h = lax.associative_scan(_linear_op, (a, bu), reverse=reverse, axis=1)
    return h


def s5_mixer(u_ctx, u_lat, b_re, b_im, c_re, c_im, d, a_re, a_im, log_dt, glu_w, glu_b, ctx_out):
    b_re, b_im, c_re, c_im = (t.astype(f32) for t in (b_re, b_im, c_re, c_im))
    (abar_f, coef_f), (abar_b, coef_b) = (s5_discretise(a_re[i], a_im[i], log_dt[i]) for i in range(2))

    def drive(u, coef):
        ug = to_heads(u, S5_GROUPS)
        bu = lax.complex(jnp.einsum('gpc,blgc->blgp', b_re, ug), jnp.einsum('gpc,blgc->blgp', b_im, ug))
        return coef * bu

    def readout(h, u):
        ug = to_heads(u, S5_GROUPS)
        y = (jnp.einsum('gcp,blgp->blgc', c_re, jnp.real(h)) - jnp.einsum('gcp,blgp->blgc', c_im, jnp.imag(h)) + to_heads(d, S5_GROUPS) * ug)
        z = jax.nn.gelu(y.reshape(u.shape))
        return z * jax.nn.sigmoid(z @ glu_w + glu_b)

    hc_f = s5_scan(abar_f, drive(u_ctx, coef_f), None, False)
    hc_b = s5_scan(abar_b, drive(u_ctx, coef_b), None, True)
    hl_f = s5_scan(abar_f, drive(u_lat, coef_f), hc_f[:, -1], False)
    hl_b = s5_scan(abar_b, drive(u_lat, coef_b), hc_b[:, 0], True)
    y_lat = readout(hl_f + hl_b, u_lat)
    y_ctx = readout(hc_f + hc_b, u_ctx) if ctx_out else None
    return y_ctx, y_lat


def chunk_gated_delta(q, k, v, g, beta, s0, emit):
    B, L, H, K = q.shape
    V = v.shape[-1]
    C = GDN_CHUNK
    N = L // C

    def chunks(t):
        return jnp.moveaxis(t.reshape((B, N, C) + t.shape[2:]), 3, 1)

    q, k, v, beta = chunks(q), chunks(k), chunks(v), chunks(beta)
    g = jnp.cumsum(chunks(g), axis=-1)
    causal = jnp.tril(jnp.ones((C, C), bool))
    strict = jnp.tril(jnp.ones((C, C), bool), k=-1)
    decay = jnp.exp(jnp.where(causal, g[..., :, None] - g[..., None, :], -jnp.inf))
    kb = k * beta[..., None]
    m = jnp.where(strict, jnp.einsum('bhnik,bhnjk->bhnij', kb, k) * decay, 0.0)
    rhs = jnp.concatenate([v * beta[..., None], kb * jnp.exp(g)[..., None]], axis=-1)
    sol = lax.linalg.triangular_solve(m + jnp.eye(C, dtype=m.dtype), rhs, left_side=True, lower=True, unit_diagonal=True)
    u, w = sol[..., :V], sol[..., V:]
    g_last = g[..., -1]
    kd = k * jnp.exp(g_last[..., None] - g)[..., None]
    to_front = lambda t: jnp.moveaxis(t, 2, 0)
    if emit:
        qg = q * jnp.exp(g)[..., None]
        attn = jnp.einsum('bhnik,bhnjk->bhnij', q, k) * decay
        xs = tuple(to_front(t) for t in (kd, u, w, g_last, qg, attn))
    else:
        xs = tuple(to_front(t) for t in (kd, u, w, g_last))

    def step(s, inp):
        kd_n, u_n, w_n, gl_n = inp[:4]
        v_new = u_n - jnp.einsum('bhck,bhkv->bhcv', w_n, s)
        s_next = s * jnp.exp(gl_n)[..., None, None] + jnp.einsum('bhck,bhcv->bhkv', kd_n, v_new)
        if emit:
            qg_n, attn_n = inp[4:]
            o = jnp.einsum('bhck,bhkv->bhcv', qg_n, s) + jnp.einsum('bhij,bhjv->bhiv', attn_n, v_new)
            return s_next, o
        return s_next, None

    s_final, o = lax.scan(step, s0, xs)
    if emit:
        o = jnp.moveaxis(jnp.moveaxis(o, 0, 2), 1, 3).reshape(B, L, H, V)
    return o, s_final


def gdn_prep(p, conv_w, a_log, dt_bias):
    W, H = GDN_WIDTH, GDN_HEADS
    qkv = jax.nn.silu(centred_dwconv(p[..., :3 * W], conv_w))
    q = l2norm(to_heads(qkv[..., :W], H)) * GDN_HEAD ** -0.5
    k = l2norm(to_heads(qkv[..., W:2 * W], H))
    v = to_heads(qkv[..., 2 * W:], H)
    z = to_heads(p[..., 3 * W:4 * W], H)
    beta = jax.nn.sigmoid(to_heads(p[..., 4 * W:4 * W + 2 * H], 2))
    g = -jnp.exp(a_log) * jax.nn.softplus(to_heads(p[..., 4 * W + 2 * H:], 2) + dt_bias)
    return q, k, v, z, beta, g


def gdn_mixer(p_ctx, p_lat, conv_w, a_log, dt_bias, norm_w, ctx_out):
    qc, kc, vc, zc, bc, gc = gdn_prep(p_ctx, conv_w, a_log, dt_bias)
    ql, kl, vl, zl, bl, gl = gdn_prep(p_lat, conv_w, a_log, dt_bias)
    s0 = jnp.zeros((p_lat.shape[0], GDN_HEADS, GDN_HEAD, GDN_HEAD), f32)
    rev = lambda t: jnp.flip(t, axis=1)
    oc_f, sc_f = chunk_gated_delta(qc, kc, vc, gc[:, :, 0], bc[:, :, 0], s0, ctx_out)
    oc_b, sc_b = chunk_gated_delta(rev(qc), rev(kc), rev(vc), rev(gc[:, :, 1]), rev(bc[:, :, 1]), s0, ctx_out)
    ol_f, _ = chunk_gated_delta(ql, kl, vl, gl[:, :, 0], bl[:, :, 0], sc_f, True)
    ol_b, _ = chunk_gated_delta(rev(ql), rev(kl), rev(vl), rev(gl[:, :, 1]), rev(bl[:, :, 1]), sc_b, True)

    def readout(o, z):
        return (rmsnorm(o, norm_w) * jax.nn.silu(z)).reshape(z.shape[0], z.shape[1], GDN_WIDTH)

    y_lat = readout(ol_f + rev(ol_b), zl)
    y_ctx = readout(oc_f + rev(oc_b), zc) if ctx_out else None
    return y_ctx, y_lat


def rwkv_prep(p, shifted, mu, w0, w_up, a0, a_up, k_k, k_a):
    B, L, _ = p.shape
    W, H = RWKV_WIDTH, RWKV_HEADS
    p = p + (shifted - p) * mu
    r, k, v = p[..., :W], p[..., W:2 * W], p[..., 2 * W:3 * W]
    o1 = 3 * W
    o2 = o1 + 2 * DECAY_LORA
    o3 = o2 + 2 * ICLR_LORA
    w_dn = p[..., o1:o2].reshape(B, L, 2, DECAY_LORA)
    a_dn = p[..., o2:o3].reshape(B, L, 2, ICLR_LORA)
    g_dn = p[..., o3:]
    w_log = -jax.nn.softplus(-(w0 + jnp.einsum('bldr,drw->bldw', jnp.tanh(w_dn), w_up))) - 0.5
    decay = jnp.exp(-jnp.exp(w_log))
    iclr = jax.nn.sigmoid(a0 + jnp.einsum('bldr,drw->bldw', a_dn, a_up))
    kk = l2norm(to_heads(k * k_k, H))
    k_dir = k[:, :, None, :] * (1.0 + (iclr - 1.0) * k_a)
    b_dir = kk.reshape(B, L, 1, W) * iclr
    heads = lambda t: to_heads(t, H)
    return heads(r), heads(v), kk, heads(decay), heads(k_dir), heads(b_dir), g_dn


def rwkv7_scan(r, decay, k, v, kk, b, s0, reverse, emit):
    def step(s, inp):
        r_t, w_t, k_t, v_t, kk_t, b_t = inp
        sa = jnp.einsum('bhvk,bhk->bhv', s, kk_t)
        s = s * w_t[:, :, None, :] - sa[..., None] * b_t[:, :, None, :] + v_t[..., None] * k_t[:, :, None, :]
        return s, (jnp.einsum('bhvk,bhk->bhv', s, r_t) if emit else None)

    xs = tuple(jnp.moveaxis(t, 1, 0) for t in (r, decay, k, v, kk, b))
    s_final, y = lax.scan(step, s0, xs, reverse=reverse)
    return (jnp.moveaxis(y, 0, 1) if emit else None), s_final


def rwkv_mixer(p_ctx, p_lat, mu, w0, w_up, a0, a_up, g_up, k_k, k_a, r_k, ln_w, ln_b, ctx_out):
    args = (mu, w0, w_up, a0, a_up, k_k, k_a)
    ctx_t = rwkv_prep(p_ctx, seq_shift(p_ctx), *args)
    lat_t = rwkv_prep(p_lat, grid_qshift(p_lat), *args)
    s0 = jnp.zeros((p_lat.shape[0], RWKV_HEADS, RWKV_HEAD, RWKV_HEAD), f32)

    def bidir(t, s0_f, s0_b, emit):
        r, v, kk, decay, k_dir, b_dir, _ = t
        y_f, s_f = rwkv7_scan(r, decay[:, :, 0], k_dir[:, :, 0], v, kk, b_dir[:, :, 0], s0_f, False, emit)
        y_b, s_b = rwkv7_scan(r, decay[:, :, 1], k_dir[:, :, 1], v, kk, b_dir[:, :, 1], s0_b, True, emit)
        return y_f, y_b, s_f, s_b

    def readout(y, t):
        r, v, _, _, k_dir, _, g_dn = t
        B, L = y.shape[:2]
        mean = jnp.mean(y, axis=-1, keepdims=True)
        var = jnp.mean(jnp.square(y - mean), axis=-1, keepdims=True)
        yn = ((y - mean) * lax.rsqrt(var + RWKV_GN_EPS)).reshape(B, L, RWKV_WIDTH) * ln_w + ln_b
        bonus = jnp.sum(r[:, :, None] * k_dir * r_k, axis=(2, 4))[..., None] * v
        gate = jax.nn.sigmoid(g_dn) @ g_up
        return (yn + bonus.reshape(B, L, RWKV_WIDTH)) * gate

    yc_f, yc_b, sc_f, sc_b = bidir(ctx_t, s0, s0, ctx_out)
    yl_f, yl_b, _, _ = bidir(lat_t, sc_f, sc_b, True)
    y_lat = readout(yl_f + yl_b, lat_t)
    y_ctx = readout(yc_f + yc_b, ctx_t) if ctx_out else None
    return y_ctx, y_lat


def setup_inputs(seed: int = 0) -> dict:
    key = jax.random.key(seed)
    ks = iter(jax.random.split(key, 64))

    def nrm(shape, scale):
        return scale * jax.random.normal(next(ks), shape, f32)

    def unif(shape, lo, hi):
        return jax.random.uniform(next(ks), shape, f32, lo, hi)

    G, P, H = S5_GROUPS, S5_STATE, GDN_HEADS
    gdn_dt = unif((DEPTH, 2, H), 1e-3, 1e-1)
    return {
        'x': nrm((BATCH, SEQ, D_MODEL), 1.0),
        'c': nrm((BATCH, D_MODEL), 1.0),
        'ctx': nrm((BATCH, CTX_LEN, D_MODEL), 1.0),
        'c_ctx': nrm((D_MODEL,), 1.0),
        'mod_w': nrm((DEPTH, D_MODEL, N_MOD * D_MODEL), 0.5 * D_MODEL ** -0.5),
        'mod_b': nrm((DEPTH, N_MOD * D_MODEL), 0.01),
        'norm_mix': 1.0 + nrm((DEPTH, D_MODEL), 0.02),
        'norm_mlp': 1.0 + nrm((DEPTH, D_MODEL), 0.02),
        'norm_final': 1.0 + nrm((D_MODEL,), 0.02),
        'w_in': nrm((DEPTH, D_MODEL, IN_COLS), D_MODEL ** -0.5),
        'w_out': nrm((DEPTH, D_MIX, D_MODEL), D_MIX ** -0.5),
        's5_b_re': nrm((DEPTH, G, P, S5_GROUP), (2 * S5_GROUP) ** -0.5),
        's5_b_im': nrm((DEPTH, G, P, S5_GROUP), (2 * S5_GROUP) ** -0.5),
        's5_c_re': nrm((DEPTH, G, S5_GROUP, P), P ** -0.5),
        's5_c_im': nrm((DEPTH, G, S5_GROUP, P), P ** -0.5),
        's5_d': nrm((DEPTH, S5_WIDTH), 0.5),
        's5_a_re': -0.5 + nrm((DEPTH, 2, G, P), 0.01),
        's5_a_im': jnp.pi * jnp.arange(P, dtype=f32) + nrm((DEPTH, 2, G, P), 0.01),
        's5_log_dt': unif((DEPTH, 2, G), math.log(1e-3), math.log(1e-1)),
        's5_glu_w': nrm((DEPTH, S5_WIDTH, S5_WIDTH), S5_WIDTH ** -0.5),
        's5_glu_b': nrm((DEPTH, S5_WIDTH), 0.01),
        'gdn_conv': nrm((DEPTH, GDN_CONV, 3 * GDN_WIDTH), GDN_CONV ** -0.5),
        'gdn_a_log': jnp.log(unif((DEPTH, 2, H), 1.0, 16.0)),
        'gdn_dt_bias': gdn_dt + jnp.log(-jnp.expm1(-gdn_dt)),
        'gdn_norm': 1.0 + nrm((DEPTH, GDN_HEAD), 0.02),
        'rwkv_mu': unif((DEPTH, RWKV_COLS), 0.0, 1.0),
        'rwkv_w0': unif((DEPTH, 2, RWKV_WIDTH), -5.0, 1.0),
        'rwkv_w_up': nrm((DEPTH, 2, DECAY_LORA, RWKV_WIDTH), 0.1),
        'rwkv_a0': nrm((DEPTH, 2, RWKV_WIDTH), 0.1),
        'rwkv_a_up': nrm((DEPTH, 2, ICLR_LORA, RWKV_WIDTH), 0.1),
        'rwkv_g_up': nrm((DEPTH, GATE_LORA, RWKV_WIDTH), GATE_LORA ** -0.5),
        'rwkv_k_k': 0.85 + nrm((DEPTH, RWKV_WIDTH), 0.02),
        'rwkv_k_a': 1.0 + nrm((DEPTH, RWKV_WIDTH), 0.02),
        'rwkv_r_k': nrm((DEPTH, RWKV_HEADS, RWKV_HEAD), 0.1),
        'rwkv_ln_w': 1.0 + nrm((DEPTH, RWKV_WIDTH), 0.02),
        'rwkv_ln_b': nrm((DEPTH, RWKV_WIDTH), 0.01),
        'mlp_w1': nrm((DEPTH, D_MODEL, D_FF), D_MODEL ** -0.5),
        'mlp_w2': nrm((DEPTH, D_FF, D_MODEL), D_FF ** -0.5),
    }


def reference(x, c, ctx, c_ctx, mod_w, mod_b, norm_mix, norm_mlp, norm_final, w_in, w_out,
              s5_b_re, s5_b_im, s5_c_re, s5_c_im, s5_d, s5_a_re, s5_a_im, s5_log_dt, s5_glu_w, s5_glu_b,
              gdn_conv, gdn_a_log, gdn_dt_bias, gdn_norm,
              rwkv_mu, rwkv_w0, rwkv_w_up, rwkv_a0, rwkv_a_up, rwkv_g_up, rwkv_k_k, rwkv_k_a, rwkv_r_k,
              rwkv_ln_w, rwkv_ln_b, mlp_w1, mlp_w2):
    o_gdn = S5_COLS
    o_rwkv = S5_COLS + GDN_COLS
    for layer in range(DEPTH):
        ctx_out = layer < DEPTH - 1
        mods = jax.nn.silu(c.astype(f32)) @ mod_w[layer] + mod_b[layer]
        sh1, sc1, gt1, sh2, sc2, gt2 = jnp.split(mods[:, None, :], N_MOD, axis=-1)
        cmods = jnp.split(jax.nn.silu(c_ctx.astype(f32)) @ mod_w[layer] + mod_b[layer], N_MOD)

        p_lat = (rmsnorm(x, norm_mix[layer]) * (1.0 + sc1) + sh1) @ w_in[layer]
        p_ctx = (rmsnorm(ctx, norm_mix[layer]) * (1.0 + cmods[1]) + cmods[0]) @ w_in[layer]
        ya_c, ya_l = s5_mixer(p_ctx[..., :o_gdn], p_lat[..., :o_gdn], s5_b_re[layer], s5_b_im[layer],
                              s5_c_re[layer], s5_c_im[layer], s5_d[layer], s5_a_re[layer], s5_a_im[layer],
                              s5_log_dt[layer], s5_glu_w[layer], s5_glu_b[layer], ctx_out)
        yb_c, yb_l = gdn_mixer(p_ctx[..., o_gdn:o_rwkv], p_lat[..., o_gdn:o_rwkv], gdn_conv[layer],
                               gdn_a_log[layer], gdn_dt_bias[layer], gdn_norm[layer], ctx_out)
        yc_c, yc_l = rwkv_mixer(p_ctx[..., o_rwkv:], p_lat[..., o_rwkv:], rwkv_mu[layer], rwkv_w0[layer],
                                rwkv_w_up[layer], rwkv_a0[layer], rwkv_a_up[layer], rwkv_g_up[layer],
                                rwkv_k_k[layer], rwkv_k_a[layer], rwkv_r_k[layer], rwkv_ln_w[layer],
                                rwkv_ln_b[layer], ctx_out)
        x = x + gt1 * (jnp.concatenate([ya_l, yb_l, yc_l], axis=-1) @ w_out[layer])

        h = rmsnorm(x, norm_mlp[layer]) * (1.0 + sc2) + sh2
        x = x + gt2 * sq_relu_mlp(h, mlp_w1[layer], mlp_w2[layer])

        if ctx_out:
            ctx = ctx + cmods[2] * (jnp.concatenate([ya_c, yb_c, yc_c], axis=-1) @ w_out[layer])
            hc = rmsnorm(ctx, norm_mlp[layer]) * (1.0 + cmods[4]) + cmods[3]
            ctx = ctx + cmods[5] * sq_relu_mlp(hc, mlp_w1[layer], mlp_w2[layer])
    return rmsnorm(x, norm_final)
```

```python
import functools
import math

import jax
import jax.numpy as jnp
from jax import lax
from jax.experimental import pallas as pl
from jax.experimental.pallas import tpu as pltpu

F32 = jnp.float32
BF16 = jnp.bfloat16
HIGHEST = lax.Precision.HIGHEST

D_MODEL = 1024
D_FF = 4 * D_MODEL
N_MOD = 6
EPS = 1e-6
GRID_W = 64

S5_WIDTH = 256
S5_GROUP = 16
S5_GROUPS = 16
S5_STATE = 64
S5_N = S5_GROUPS * S5_STATE

HEAD = 64
HEADS = 6
WIDTH = HEADS * HEAD
GDN_CONV = 5
GDN_TAIL = 128
GDN_COLS_PACKED = 4 * WIDTH + GDN_TAIL
LORA = 64
GATE_LORA = 128
RWKV_COLS = 3 * WIDTH + 2 * LORA + 2 * LORA + GATE_LORA
RWKV_GN_EPS = 64e-5
CHUNK = 64
SUBLANES = 8

ROW_TILE = 256


def _dot(a, b, dims=((1,), (0,))):
    return lax.dot_general(a.astype(BF16), b.astype(BF16), (dims, ((), ())),
                           preferred_element_type=F32)


def _dot_f32(a, b, dims=((1,), (0,))):
    return lax.dot_general(a, b, (dims, ((), ())), precision=HIGHEST,
                           preferred_element_type=F32)


_NT = ((1,), (1,))
_TN = ((0,), (0,))


def _sigmoid(x):
    return 1.0 / (1.0 + jnp.exp(-x))


def _silu(x):
    return x * _sigmoid(x)


def _softplus(x):
    return jnp.maximum(x, 0.0) + jnp.log1p(jnp.exp(-jnp.abs(x)))


def _tri_mask(n, rev, strict):
    r = lax.broadcasted_iota(jnp.int32, (n, n), 0)
    c = lax.broadcasted_iota(jnp.int32, (n, n), 1)
    if rev:
        return (c > r) if strict else (c >= r)
    return (c < r) if strict else (c <= r)


def _inv_unit_tri(a):
    n = a.shape[0]
    r = lax.broadcasted_iota(jnp.int32, (n, n), 0)
    c = lax.broadcasted_iota(jnp.int32, (n, n), 1)
    x = -a
    inv = jnp.where(r == c, 1.0, 0.0).astype(F32) + x
    p = x
    for _ in range(int(math.log2(n)) - 1):
        p = _dot(p, p)
        inv = inv + _dot(inv, p)
    return inv


def _head_sums(x):
    return [jnp.sum(x[:, h * HEAD:(h + 1) * HEAD], axis=-1, keepdims=True) for h in range(HEADS)]


def _head_bcast(cols, shape):
    hid = lax.broadcasted_iota(jnp.int32, shape, 1) >> int(math.log2(HEAD))
    out = jnp.broadcast_to(cols[HEADS - 1], shape)
    for h in range(HEADS - 2, -1, -1):
        out = jnp.where(hid == h, cols[h], out)
    return out


def _shift_rows(x, halo, offset):
    n = x.shape[0]
    row = lax.broadcasted_iota(jnp.int32, x.shape, 0)
    if offset < 0:
        k = -offset
        return jnp.where(row < k, pltpu.roll(halo, k, 0), pltpu.roll(x, k, 0))
    k = n - offset
    return jnp.where(row >= k, pltpu.roll(halo, k, 0), pltpu.roll(x, k, 0))


def _chunk_index(j, n_chunks, rev):
    return (n_chunks - 1 - j) if rev else j


def _cparams(sem):
    return pltpu.CompilerParams(dimension_semantics=sem)


def _mods_kernel(c_ref, w_ref, b_ref, o_ref):
    o_ref[0] = _dot_f32(_silu(c_ref[...]), w_ref[0]) + b_ref[0]


def _mods(cvec, mod_w, mod_b):
    depth = mod_w.shape[0]
    rows = cvec.shape[0]
    n_tile = 1536
    return pl.pallas_call(
        _mods_kernel,
        grid=(depth, N_MOD * D_MODEL // n_tile),
        in_specs=[
            pl.BlockSpec((rows, D_MODEL), lambda l, n: (0, 0)),
            pl.BlockSpec((1, D_MODEL, n_tile), lambda l, n: (l, 0, n)),
            pl.BlockSpec((1, 1, n_tile), lambda l, n: (l, 0, n)),
        ],
        out_specs=pl.BlockSpec((1, rows, n_tile), lambda l, n: (l, 0, n)),
        out_shape=jax.ShapeDtypeStruct((depth, rows, N_MOD * D_MODEL), F32),
        compiler_params=_cparams(("arbitrary", "arbitrary")),
        name="mods",
    )(cvec, mod_w, mod_b.reshape(depth, 1, N_MOD * D_MODEL))


def _inproj_kernel(x_ref, g_ref, sc_ref, sh_ref, w_ref, oa_ref, ob_ref, oc_ref):
    x = x_ref[0]
    xn = x * lax.rsqrt(jnp.mean(x * x, axis=-1, keepdims=True) + EPS) * g_ref[...]
    xm = xn * (1.0 + sc_ref[0]) + sh_ref[0]
    p = _dot(xm, w_ref[...])
    oa_ref[0] = p[:, :S5_WIDTH]
    ob_ref[0] = p[:, S5_WIDTH:S5_WIDTH + GDN_COLS_PACKED]
    oc_ref[0] = p[:, S5_WIDTH + GDN_COLS_PACKED:]


def _inproj(x, gain, sc, sh, w_packed):
    b, l, _ = x.shape
    tm = min(ROW_TILE, l)
    cols = w_packed.shape[1]
    vec = pl.BlockSpec((1, 1, D_MODEL), lambda i, t: (i, 0, 0))
    return pl.pallas_call(
        _inproj_kernel,
        grid=(b, l // tm),
        in_specs=[
            pl.BlockSpec((1, tm, D_MODEL), lambda i, t: (i, t, 0)),
            pl.BlockSpec((1, D_MODEL), lambda i, t: (0, 0)),
            vec, vec,
            pl.BlockSpec((D_MODEL, cols), lambda i, t: (0, 0)),
        ],
        out_specs=[
            pl.BlockSpec((1, tm, S5_WIDTH), lambda i, t: (i, t, 0)),
            pl.BlockSpec((1, tm, GDN_COLS_PACKED), lambda i, t: (i, t, 0)),
            pl.BlockSpec((1, tm, RWKV_COLS), lambda i, t: (i, t, 0)),
        ],
        out_shape=[
            jax.ShapeDtypeStruct((b, l, S5_WIDTH), F32),
            jax.ShapeDtypeStruct((b, l, GDN_COLS_PACKED), F32),
            jax.ShapeDtypeStruct((b, l, RWKV_COLS), F32),
        ],
        compiler_params=_cparams(("parallel", "arbitrary")),
        name="inproj",
    )(x, gain.reshape(1, D_MODEL), sc, sh, w_packed)


def _outproj_kernel(x_ref, ya_ref, yb_ref, yc_ref, woa_ref, wob_ref, woc_ref, gt1_ref,
                    g_ref, sc_ref, sh_ref, gt2_ref, w1_ref, w2_ref, gf_ref, o_ref, *, final):
    mix = (_dot(ya_ref[0], woa_ref[...]) + _dot(yb_ref[0], wob_ref[...])
           + _dot(yc_ref[0], woc_ref[...]))
    x1 = x_ref[0] + gt1_ref[0] * mix
    h = x1 * lax.rsqrt(jnp.mean(x1 * x1, axis=-1, keepdims=True) + EPS) * g_ref[...]
    h = h * (1.0 + sc_ref[0]) + sh_ref[0]
    a = jnp.maximum(_dot(h, w1_ref[...]), 0.0)
    x2 = x1 + gt2_ref[0] * _dot(a * a, w2_ref[...])
    if final:
        x2 = x2 * lax.rsqrt(jnp.mean(x2 * x2, axis=-1, keepdims=True) + EPS) * gf_ref[...]
    o_ref[0] = x2


def _outproj(x, ya, yb, yc, wo_parts, gt1, gain, sc, sh, gt2, w1, w2, gain_final, final):
    b, l, _ = x.shape
    tm = min(ROW_TILE, l)
    vec = pl.BlockSpec((1, 1, D_MODEL), lambda i, t: (i, 0, 0))
    row = pl.BlockSpec((1, D_MODEL), lambda i, t: (0, 0))

    def tok(width):
        return pl.BlockSpec((1, tm, width), lambda i, t: (i, t, 0))

    def full(arr):
        return pl.BlockSpec(arr.shape, lambda i, t: (0, 0))

    woa, wob, woc = wo_parts
    return pl.pallas_call(
        functools.partial(_outproj_kernel, final=final),
        grid=(b, l // tm),
        in_specs=[tok(D_MODEL), tok(S5_WIDTH), tok(WIDTH), tok(WIDTH),
                  full(woa), full(wob), full(woc), vec, row, vec, vec, vec,
                  full(w1), full(w2), row],
        out_specs=tok(D_MODEL),
        out_shape=jax.ShapeDtypeStruct((b, l, D_MODEL), F32),
        compiler_params=_cparams(("parallel", "arbitrary")),
        name="outproj_mlp",
    )(x, ya, yb, yc, woa, wob, woc, gt1, gain.reshape(1, D_MODEL), sc, sh, gt2, w1, w2,
      gain_final.reshape(1, D_MODEL))


def _cmul(a, b):
    return a[0] * b[0] - a[1] * b[1], a[0] * b[1] + a[1] * b[0]


def _s5_kernel(*refs, rev, readout, n_chunks):
    (u_ref, are_ref, aim_ref, ldt_ref, bre_ref, bim_ref, cre_ref, cim_ref,
     h0re_ref, h0im_ref) = refs[:10]
    if readout:
        d_ref, gw_ref, gb_ref, yf_ref = refs[10:14]
        y_ref, hre_ref, him_ref = refs[14:17]
        wre, wim, lvl, pw, carry = refs[17:]
    else:
        y_ref, hre_ref, him_ref = refs[10:13]
        wre, wim, lvl, pw, carry = refs[13:]
    t = u_ref.shape[1]
    j = pl.program_id(1)
    tile = (SUBLANES, S5_N)

    @pl.when(j == 0)
    def _discretise():
        dt = jnp.exp(ldt_ref[...])
        a_re = are_ref[...]
        a_im = aim_ref[...]
        mag = jnp.exp(a_re * dt)
        ab = (mag * jnp.cos(a_im * dt), mag * jnp.sin(a_im * dt))
        den = a_re * a_re + a_im * a_im
        xr = ab[0] - 1.0
        cr = (xr * a_re + ab[1] * a_im) / den
        ci = (ab[1] * a_re - xr * a_im) / den
        wre[...] = (bre_ref[...] * cr - bim_ref[...] * ci).astype(BF16)
        wim[...] = (bre_ref[...] * ci + bim_ref[...] * cr).astype(BF16)
        powers = [ab]
        for _ in range(SUBLANES - 1):
            powers.append(_cmul(powers[-1], ab))
        row = lax.broadcasted_iota(jnp.int32, tile, 0)
        for li, s in enumerate((1, 2, 4)):
            keep = (row < SUBLANES - s) if rev else (row >= s)
            lvl[2 * li] = jnp.where(keep, powers[s - 1][0], 0.0)
            lvl[2 * li + 1] = jnp.where(keep, powers[s - 1][1], 0.0)
        pr = jnp.zeros(tile, F32)
        pi = jnp.zeros(tile, F32)
        for r in range(SUBLANES):
            k = (SUBLANES - 1 - r) if rev else r
            pr = jnp.where(row == r, powers[k][0], pr)
            pi = jnp.where(row == r, powers[k][1], pi)
        pw[0] = pr
        pw[1] = pi
        carry[0:1, :] = h0re_ref[0]
        carry[1:2, :] = h0im_ref[0]

    u = u_ref[0]
    bur = _dot(u, wre[...])
    bui = _dot(u, wim[...])
    c_re = carry[0:1, :]
    c_im = carry[1:2, :]
    n_tiles = t // SUBLANES
    h_re = [None] * n_tiles
    h_im = [None] * n_tiles
    for i in (range(n_tiles - 1, -1, -1) if rev else range(n_tiles)):
        tr = bur[i * SUBLANES:(i + 1) * SUBLANES]
        ti = bui[i * SUBLANES:(i + 1) * SUBLANES]
        for li, s in enumerate((1, 2, 4)):
            sh = (SUBLANES - s) if rev else s
            sr = pltpu.roll(tr, sh, 0)
            si = pltpu.roll(ti, sh, 0)
            lr = lvl[2 * li]
            lim = lvl[2 * li + 1]
            tr, ti = tr + lr * sr - lim * si, ti + lr * si + lim * sr
        cbr = jnp.broadcast_to(c_re, tile)
        cbi = jnp.broadcast_to(c_im, tile)
        hr = tr + pw[0] * cbr - pw[1] * cbi
        hi = ti + pw[0] * cbi + pw[1] * cbr
        edge = 0 if rev else SUBLANES - 1
        c_re = hr[edge:edge + 1]
        c_im = hi[edge:edge + 1]
        h_re[i] = hr
        h_im[i] = hi
    carry[0:1, :] = c_re
    carry[1:2, :] = c_im
    y = (_dot(jnp.concatenate(h_re, axis=0), cre_ref[...])
         - _dot(jnp.concatenate(h_im, axis=0), cim_ref[...]))
    if readout:
        y = y + yf_ref[0] + d_ref[...] * u
        z = 0.5 * y * (1.0 + jnp.tanh(math.sqrt(2.0 / math.pi) * (y + 0.044715 * y * y * y)))
        y = z * _sigmoid(_dot(z, gw_ref[...]) + gb_ref[...])
    y_ref[0] = y

    @pl.when(j == n_chunks - 1)
    def _emit_state():
        hre_ref[0] = c_re
        him_ref[0] = c_im


def _s5_call(u, prm, d, h0, rev, readout, yf=None):
    b, l, _ = u.shape
    t = CHUNK
    n_chunks = l // t
    tok = pl.BlockSpec((1, t, S5_WIDTH), lambda i, j: (i, _chunk_index(j, n_chunks, rev), 0))
    row_n = pl.BlockSpec((1, S5_N), lambda i, j: (0, 0))
    st = pl.BlockSpec((1, 1, S5_N), lambda i, j: (i, 0, 0))

    def full(arr):
        return pl.BlockSpec(arr.shape, lambda i, j: (0,) * arr.ndim)

    args = [u, prm["a_re"][d], prm["a_im"][d], prm["log_dt"][d], prm["b_re"], prm["b_im"],
            prm["c_re"], prm["c_im"], h0[0], h0[1]]
    specs = [tok, row_n, row_n, row_n, full(prm["b_re"]), full(prm["b_im"]),
             full(prm["c_re"]), full(prm["c_im"]), st, st]
    if readout:
        args += [prm["d"], prm["glu_w"], prm["glu_b"], yf]
        specs += [full(prm["d"]), full(prm["glu_w"]), full(prm["glu_b"]), tok]
    y, hre, him = pl.pallas_call(
        functools.partial(_s5_kernel, rev=rev, readout=readout, n_chunks=n_chunks),
        grid=(b, n_chunks),
        in_specs=specs,
        out_specs=[tok, st, st],
        out_shape=[jax.ShapeDtypeStruct((b, l, S5_WIDTH), F32),
                   jax.ShapeDtypeStruct((b, 1, S5_N), F32),
                   jax.ShapeDtypeStruct((b, 1, S5_N), F32)],
        scratch_shapes=[pltpu.VMEM((S5_WIDTH, S5_N), BF16), pltpu.VMEM((S5_WIDTH, S5_N), BF16),
                        pltpu.VMEM((6, SUBLANES, S5_N), F32), pltpu.VMEM((2, SUBLANES, S5_N), F32),
                        pltpu.VMEM((2, S5_N), F32)],
        compiler_params=_cparams(("parallel", "arbitrary")),
        name="s5_bwd" if rev else "s5_fwd",
    )(*args)
    return y, (hre, him)


def _s5_mixer(u_ctx, u_lat, prm):
    b = u_lat.shape[0]
    zero = (jnp.zeros((b, 1, S5_N), F32), jnp.zeros((b, 1, S5_N), F32))
    yc_f, hc_f = _s5_call(u_ctx, prm, 0, zero, False, False)
    y_ctx, hc_b = _s5_call(u_ctx, prm, 1, zero, True, True, yc_f)
    yl_f, _ = _s5_call(u_lat, prm, 0, hc_f, False, False)
    y_lat, _ = _s5_call(u_lat, prm, 1, hc_b, True, True, yl_f)
    return y_ctx, y_lat


def _gdn_kernel(*refs, rev, readout, n_chunks, d):
    pp_ref, pc_ref, pn_ref, cw_ref, alog_ref, dtb_ref, s0_ref = refs[:7]
    if readout:
        nw_ref, of_ref = refs[7:9]
        y_ref, so_ref = refs[9:11]
        state, obuf = refs[11:]
    else:
        y_ref, so_ref = refs[7:9]
        state, obuf = refs[9:]
    t = pc_ref.shape[1]
    j = pl.program_id(1)
    c = _chunk_index(j, n_chunks, rev)

    @pl.when(j == 0)
    def _load_state():
        state[...] = s0_ref[0]

    x = pc_ref[0]
    qkv = x[:, :3 * WIDTH]
    prev = jnp.where(c > 0, pp_ref[0][:, :3 * WIDTH], 0.0)
    nxt = jnp.where(c < n_chunks - 1, pn_ref[0][:, :3 * WIDTH], 0.0)
    half = GDN_CONV // 2
    acc = cw_ref[half:half + 1, :] * qkv
    for off in range(-half, half + 1):
        if off != 0:
            acc = acc + cw_ref[half + off:half + off + 1, :] * _shift_rows(
                qkv, prev if off < 0 else nxt, off)
    act = _silu(acc)
    q = act[:, :WIDTH]
    k = act[:, WIDTH:2 * WIDTH]
    v = act[:, 2 * WIDTH:]
    q = q * _head_bcast([lax.rsqrt(s + EPS) * HEAD ** -0.5 for s in _head_sums(q * q)], q.shape)
    k = k * _head_bcast([lax.rsqrt(s + EPS) for s in _head_sums(k * k)], k.shape)
    tail = x[:, 4 * WIDTH:]
    beta_all = _sigmoid(tail)
    g_all = -jnp.exp(alog_ref[...]) * _softplus(tail + dtb_ref[...])
    incl = _tri_mask(t, rev, False)
    strict = _tri_mask(t, rev, True)
    gcs = _dot_f32(jnp.where(incl, 1.0, 0.0).astype(F32), g_all)
    gcs_t = gcs.T
    edge = 0 if rev else t - 1
    for h in range(HEADS):
        sl = slice(h * HEAD, (h + 1) * HEAD)
        col = 2 * HEADS + d * HEADS + h
        gc = gcs[:, col:col + 1]
        gr = gcs_t[col:col + 1, :]
        decay = jnp.where(incl, jnp.exp(jnp.minimum(gc - gr, 0.0)), 0.0)
        bh = beta_all[:, d * HEADS + h:d * HEADS + h + 1]
        kh = k[:, sl]
        qh = q[:, sl]
        vh = v[:, sl]
        kb = kh * bh
        big = _dot(jnp.concatenate([kb, qh], axis=0), kh, _NT)
        m = jnp.where(strict, big[:t] * decay, 0.0)
        attn = big[t:] * decay
        tinv = _inv_unit_tri(m)
        eg = jnp.exp(gc)
        u = _dot(tinv, vh * bh)
        w = _dot(tinv, kb * eg)
        g_last = gc[edge:edge + 1]
        kd = kh * jnp.exp(g_last - gc)
        q_eff = qh * eg - _dot(attn, w)
        y_in = _dot(attn, u)
        s_prev = state[h]
        obuf[:, sl] = _dot(q_eff, s_prev) + y_in
        state[h] = (s_prev * jnp.exp(g_last) - _dot(_dot(kd, w, _TN), s_prev)
                    + _dot(kd, u, _TN))
    o = obuf[...]
    if readout:
        o = o + of_ref[0]
        z = x[:, 3 * WIDTH:4 * WIDTH]
        scale = _head_bcast([lax.rsqrt(s * (1.0 / HEAD) + EPS) for s in _head_sums(o * o)],
                            o.shape)
        o = o * scale * nw_ref[...] * _silu(z)
    y_ref[0] = o

    @pl.when(j == n_chunks - 1)
    def _emit_state():
        so_ref[0] = state[...]


def _halo_specs(width, n_chunks, rev):
    def cur(i, j):
        return (i, _chunk_index(j, n_chunks, rev), 0)

    def prev(i, j):
        return (i, jnp.maximum(_chunk_index(j, n_chunks, rev) - 1, 0), 0)

    def nxt(i, j):
        return (i, jnp.minimum(_chunk_index(j, n_chunks, rev) + 1, n_chunks - 1), 0)

    blk = (1, CHUNK, width)
    return [pl.BlockSpec(blk, prev), pl.BlockSpec(blk, cur), pl.BlockSpec(blk, nxt)]


def _gdn_call(p, prm, d, s0, rev, readout, of=None):
    b, l, _ = p.shape
    n_chunks = l // CHUNK
    tok = pl.BlockSpec((1, CHUNK, WIDTH), lambda i, j: (i, _chunk_index(j, n_chunks, rev), 0))
    st = pl.BlockSpec((1, HEADS, HEAD, HEAD), lambda i, j: (i, 0, 0, 0))

    def full(arr):
        return pl.BlockSpec(arr.shape, lambda i, j: (0,) * arr.ndim)

    args = [p, p, p, prm["conv"], prm["alog"][d], prm["dtb"][d], s0]
    specs = _halo_specs(GDN_COLS_PACKED, n_chunks, rev) + [
        full(prm["conv"]), full(prm["alog"][d]), full(prm["dtb"][d]), st]
    if readout:
        args += [prm["norm"], of]
        specs += [full(prm["norm"]), tok]
    return pl.pallas_call(
        functools.partial(_gdn_kernel, rev=rev, readout=readout, n_chunks=n_chunks, d=d),
        grid=(b, n_chunks),
        in_specs=specs,
        out_specs=[tok, st],
        out_shape=[jax.ShapeDtypeStruct((b, l, WIDTH), F32),
                   jax.ShapeDtypeStruct((b, HEADS, HEAD, HEAD), F32)],
        scratch_shapes=[pltpu.VMEM((HEADS, HEAD, HEAD), F32), pltpu.VMEM((CHUNK, WIDTH), F32)],
        compiler_params=_cparams(("parallel", "arbitrary")),
        name="gdn_bwd" if rev else "gdn_fwd",
    )(*args)


def _gdn_mixer(p_ctx, p_lat, prm):
    b = p_lat.shape[0]
    zero = jnp.zeros((b, HEADS, HEAD, HEAD), F32)
    oc_f, sc_f = _gdn_call(p_ctx, prm, 0, zero, False, False)
    y_ctx, sc_b = _gdn_call(p_ctx, prm, 1, zero, True, True, oc_f)
    ol_f, _ = _gdn_call(p_lat, prm, 0, sc_f, False, False)
    y_lat, _ = _gdn_call(p_lat, prm, 1, sc_b, True, True, ol_f)
    return y_ctx, y_lat


def _rwkv_kernel(*refs, rev, readout, grid_shift, n_chunks):
    (pp_ref, pc_ref, pn_ref, mu_ref, w0_ref, wup_ref, a0_ref, aup_ref, kk_ref, ka_ref, rk_ref,
     s0_ref) = refs[:12]
    if readout:
        gup_ref, lnw_ref, lnb_ref, yf_ref, bf_ref = refs[12:17]
        y_ref, so_ref = refs[17:19]
        state, obuf = refs[19:]
    else:
        y_ref, bon_ref, so_ref = refs[12:15]
        state, obuf = refs[15:]
    t = pc_ref.shape[1]
    j = pl.program_id(1)
    c = _chunk_index(j, n_chunks, rev)

    @pl.when(j == 0)
    def _load_state():
        state[...] = s0_ref[0]

    x = pc_ref[0]
    prev = jnp.where(c > 0, pp_ref[0], 0.0)
    nxt = jnp.where(c < n_chunks - 1, pn_ref[0], 0.0)
    lane = lax.broadcasted_iota(jnp.int32, x.shape, 1)
    if grid_shift:
        zero = jnp.zeros_like(x)
        left = _shift_rows(x, zero, -1)
        right = _shift_rows(x, zero, 1)
        cm = lane & 3
        shifted = jnp.where(cm == 0, left, jnp.where(cm == 1, right, jnp.where(cm == 2, prev, nxt)))
    else:
        shifted = jnp.where((lane & 1) == 0, _shift_rows(x, prev, -1), _shift_rows(x, nxt, 1))
    pm = x + (shifted - x) * mu_ref[...]
    r = pm[:, :WIDTH]
    k = pm[:, WIDTH:2 * WIDTH]
    v = pm[:, 2 * WIDTH:3 * WIDTH]
    o1 = 3 * WIDTH
    w_dn = pm[:, o1:o1 + 2 * LORA]
    a_dn = pm[:, o1 + 2 * LORA:o1 + 4 * LORA]
    g_dn = pm[:, o1 + 4 * LORA:]
    w_log = -_softplus(-(w0_ref[...] + _dot(jnp.tanh(w_dn), wup_ref[...]))) - 0.5
    lw = -jnp.exp(w_log)
    iclr = _sigmoid(a0_ref[...] + _dot(a_dn, aup_ref[...]))
    kkr = k * kk_ref[...]
    kk = kkr * _head_bcast([lax.rsqrt(s + EPS) for s in _head_sums(kkr * kkr)], kkr.shape)
    k_dir = k * (1.0 + (iclr - 1.0) * ka_ref[...])
    b_dir = kk * iclr
    bonus_cols = _head_sums(r * k_dir * rk_ref[...])

    incl = _tri_mask(t, rev, False)
    strict = _tri_mask(t, rev, True)
    g_inc = _dot_f32(jnp.where(incl, 1.0, 0.0).astype(F32), lw)
    g_exc = g_inc - lw
    edge = 0 if rev else t - 1
    g_end = g_inc[edge:edge + 1]
    e_neg = jnp.exp(-g_inc)
    e_end = jnp.exp(g_end - g_inc)
    kap_s = kk * jnp.exp(g_exc)
    r_s = r * jnp.exp(g_inc)
    b_s = b_dir * e_neg
    k_s = k_dir * e_neg
    b_e = b_dir * e_end
    k_e = k_dir * e_end
    dec_end = jnp.exp(g_end)
    for h in range(HEADS):
        sl = slice(h * HEAD, (h + 1) * HEAD)
        kap_h = kap_s[:, sl]
        r_h = r_s[:, sl]
        v_h = v[:, sl]
        big = _dot(jnp.concatenate([kap_h, r_h], axis=0),
                   jnp.concatenate([b_s[:, sl], k_s[:, sl]], axis=0), _NT)
        a_b = jnp.where(strict, big[:t, :t], 0.0)
        a_k = jnp.where(strict, big[:t, t:], 0.0)
        r_b = jnp.where(incl, big[t:, :t], 0.0)
        r_k = jnp.where(incl, big[t:, t:], 0.0)
        tinv = _inv_unit_tri(a_b)
        z_k = _dot(tinv, kap_h)
        z_v = _dot(tinv, _dot(a_k, v_h))
        q_eff = r_h - _dot(r_b, z_k)
        y_in = _dot(r_k, v_h) - _dot(r_b, z_v)
        b_eh = b_e[:, sl]
        s_prev = state[h]
        obuf[:, sl] = _dot(q_eff, s_prev, _NT) + y_in
        state[h] = (s_prev * dec_end[:, sl] - _dot(s_prev, _dot(z_k, b_eh, _TN))
                    + _dot(v_h, k_e[:, sl], _TN) - _dot(z_v, b_eh, _TN))
    y = obuf[...]
    bonus = _head_bcast(bonus_cols, v.shape) * v
    if readout:
        y = y + yf_ref[0]
        mean = _head_bcast([s * (1.0 / HEAD) for s in _head_sums(y)], y.shape)
        yc = y - mean
        var = _head_bcast([s * (1.0 / HEAD) for s in _head_sums(yc * yc)], y.shape)
        yn = yc * lax.rsqrt(var + RWKV_GN_EPS) * lnw_ref[...] + lnb_ref[...]
        gate = _dot(_sigmoid(g_dn), gup_ref[...])
        y_ref[0] = (yn + bonus + bf_ref[0]) * gate
    else:
        y_ref[0] = y
        bon_ref[0] = bonus

    @pl.when(j == n_chunks - 1)
    def _emit_state():
        so_ref[0] = state[...]


def _rwkv_call(p, prm, d, s0, rev, readout, grid_shift, yf=None, bf=None):
    b, l, _ = p.shape
    n_chunks = l // CHUNK
    tok = pl.BlockSpec((1, CHUNK, WIDTH), lambda i, j: (i, _chunk_index(j, n_chunks, rev), 0))
    st = pl.BlockSpec((1, HEADS, HEAD, HEAD), lambda i, j: (i, 0, 0, 0))

    def full(arr):
        return pl.BlockSpec(arr.shape, lambda i, j: (0,) * arr.ndim)

    small = [prm["mu"], prm["w0"][d], prm["w_up"][d], prm["a0"][d], prm["a_up"][d],
             prm["k_k"], prm["k_a"], prm["r_k"]]
    args = [p, p, p] + small + [s0]
    specs = _halo_specs(RWKV_COLS, n_chunks, rev) + [full(a) for a in small] + [st]
    tok_shape = jax.ShapeDtypeStruct((b, l, WIDTH), F32)
    st_shape = jax.ShapeDtypeStruct((b, HEADS, HEAD, HEAD), F32)
    if readout:
        extra = [prm["g_up"], prm["ln_w"], prm["ln_b"]]
        args += extra + [yf, bf]
        specs += [full(a) for a in extra] + [tok, tok]
        out_specs, out_shape = [tok, st], [tok_shape, st_shape]
    else:
        out_specs, out_shape = [tok, tok, st], [tok_shape, tok_shape, st_shape]
    return pl.pallas_call(
        functools.partial(_rwkv_kernel, rev=rev, readout=readout, grid_shift=grid_shift,
                          n_chunks=n_chunks),
        grid=(b, n_chunks),
        in_specs=specs,
        out_specs=out_specs,
        out_shape=out_shape,
        scratch_shapes=[pltpu.VMEM((HEADS, HEAD, HEAD), F32), pltpu.VMEM((CHUNK, WIDTH), F32)],
        compiler_params=_cparams(("parallel", "arbitrary")),
        name="rwkv_bwd" if rev else "rwkv_fwd",
    )(*args)


def _rwkv_mixer(p_ctx, p_lat, prm):
    b = p_lat.shape[0]
    zero = jnp.zeros((b, HEADS, HEAD, HEAD), F32)
    yc_f, bc_f, sc_f = _rwkv_call(p_ctx, prm, 0, zero, False, False, False)
    y_ctx, sc_b = _rwkv_call(p_ctx, prm, 1, zero, True, True, False, yc_f, bc_f)
    yl_f, bl_f, _ = _rwkv_call(p_lat, prm, 0, sc_f, False, False, True)
    y_lat, _ = _rwkv_call(p_lat, prm, 1, sc_b, True, True, True, yl_f, bl_f)
    return y_ctx, y_lat


def _pack_w_in(w):
    o_gdn = S5_WIDTH
    o_tail = o_gdn + 4 * WIDTH
    o_rwkv = o_tail + 4 * HEADS
    pad = jnp.zeros((w.shape[0], GDN_TAIL - 4 * HEADS), w.dtype)
    return jnp.concatenate([w[:, :o_tail], w[:, o_tail:o_rwkv], pad, w[:, o_rwkv:]], axis=1)


def _block_diag(blocks):
    g, r, c = blocks.shape
    eye = jnp.eye(g, dtype=blocks.dtype)
    return (eye[:, None, :, None] * blocks[:, :, None, :]).reshape(g * r, g * c)


def _s5_params(layer, b_re, b_im, c_re, c_im, d, a_re, a_im, log_dt, glu_w, glu_b):
    row = lambda v: v.reshape(2, 1, S5_N)
    return {
        "a_re": row(a_re[layer]), "a_im": row(a_im[layer]),
        "log_dt": row(jnp.repeat(log_dt[layer], S5_STATE, axis=-1)),
        "b_re": _block_diag(jnp.swapaxes(b_re[layer], 1, 2)),
        "b_im": _block_diag(jnp.swapaxes(b_im[layer], 1, 2)),
        "c_re": _block_diag(jnp.swapaxes(c_re[layer], 1, 2)),
        "c_im": _block_diag(jnp.swapaxes(c_im[layer], 1, 2)),
        "d": d[layer].reshape(1, S5_WIDTH), "glu_w": glu_w[layer],
        "glu_b": glu_b[layer].reshape(1, S5_WIDTH),
    }


def _gdn_params(layer, conv, a_log, dt_bias, norm):
    def tail_row(v):
        rows = []
        for d in range(2):
            z = jnp.zeros((GDN_TAIL,), F32)
            rows.append(z.at[2 * HEADS + d * HEADS:2 * HEADS + (d + 1) * HEADS].set(v[d])
                        .reshape(1, GDN_TAIL))
        return rows

    return {"conv": conv[layer], "alog": tail_row(a_log[layer]), "dtb": tail_row(dt_bias[layer]),
            "norm": jnp.tile(norm[layer], HEADS).reshape(1, WIDTH)}


def _rwkv_params(layer, mu, w0, w_up, a0, a_up, g_up, k_k, k_a, r_k, ln_w, ln_b):
    def lora_pad(w):
        z = jnp.zeros((LORA, WIDTH), F32)
        return [jnp.concatenate([w[0], z], axis=0), jnp.concatenate([z, w[1]], axis=0)]

    row = lambda v: v.reshape(1, -1)
    return {
        "mu": row(mu[layer]), "w0": [row(w0[layer, 0]), row(w0[layer, 1])],
        "w_up": lora_pad(w_up[layer]), "a0": [row(a0[layer, 0]), row(a0[layer, 1])],
        "a_up": lora_pad(a_up[layer]), "g_up": g_up[layer], "k_k": row(k_k[layer]),
        "k_a": row(k_a[layer]), "r_k": row(r_k[layer]), "ln_w": row(ln_w[layer]),
        "ln_b": row(ln_b[layer]),
    }


def kernel(x, c, ctx, c_ctx, mod_w, mod_b, norm_mix, norm_mlp, norm_final, w_in, w_out, s5_b_re, s5_b_im, s5_c_re, s5_c_im, s5_d, s5_a_re, s5_a_im, s5_log_dt, s5_glu_w, s5_glu_b, gdn_conv, gdn_a_log, gdn_dt_bias, gdn_norm, rwkv_mu, rwkv_w0, rwkv_w_up, rwkv_a0, rwkv_a_up, rwkv_g_up, rwkv_k_k, rwkv_k_a, rwkv_r_k, rwkv_ln_w, rwkv_ln_b, mlp_w1, mlp_w2):
    depth = mod_w.shape[0]
    batch = x.shape[0]
    x = x.astype(F32)
    ctx = ctx.astype(F32)
    rows = SUBLANES * ((batch + 1 + SUBLANES - 1) // SUBLANES)
    cvec = jnp.zeros((rows, D_MODEL), F32).at[:batch].set(c.astype(F32)).at[batch].set(
        c_ctx.astype(F32))
    mods_all = _mods(cvec, mod_w, mod_b)

    for layer in range(depth):
        ctx_out = layer < depth - 1
        m = mods_all[layer].reshape(rows, N_MOD, 1, D_MODEL)
        lat = [m[:batch, i] for i in range(N_MOD)]
        cm = [jnp.broadcast_to(m[batch, i][None], (batch, 1, D_MODEL)) for i in range(N_MOD)]
        w_packed = _pack_w_in(w_in[layer]).astype(BF16)
        wo = w_out[layer].astype(BF16)
        wo_parts = (wo[:S5_WIDTH], wo[S5_WIDTH:S5_WIDTH + WIDTH], wo[S5_WIDTH + WIDTH:])
        w1 = mlp_w1[layer].astype(BF16)
        w2 = mlp_w2[layer].astype(BF16)

        pa_l, pb_l, pc_l = _inproj(x, norm_mix[layer], lat[1], lat[0], w_packed)
        pa_c, pb_c, pc_c = _inproj(ctx, norm_mix[layer], cm[1], cm[0], w_packed)

        ya_c, ya_l = _s5_mixer(pa_c, pa_l, _s5_params(
            layer, s5_b_re, s5_b_im, s5_c_re, s5_c_im, s5_d, s5_a_re, s5_a_im, s5_log_dt,
            s5_glu_w, s5_glu_b))
        yb_c, yb_l = _gdn_mixer(pb_c, pb_l, _gdn_params(
            layer, gdn_conv, gdn_a_log, gdn_dt_bias, gdn_norm))
        yc_c, yc_l = _rwkv_mixer(pc_c, pc_l, _rwkv_params(
            layer, rwkv_mu, rwkv_w0, rwkv_w_up, rwkv_a0, rwkv_a_up, rwkv_g_up, rwkv_k_k,
            rwkv_k_a, rwkv_r_k, rwkv_ln_w, rwkv_ln_b))

        x = _outproj(x, ya_l, yb_l, yc_l, wo_parts, lat[2], norm_mlp[layer], lat[4], lat[3],
                     lat[5], w1, w2, norm_final, final=not ctx_out)
        if ctx_out:
            ctx = _outproj(ctx, ya_c, yb_c, yc_c, wo_parts, cm[2], norm_mlp[layer], cm[4], cm[3],
                           cm[5], w1, w2, norm_final, final=False)
    return x
```

```python
import functools
import math

import jax
import jax.numpy as jnp
from jax import lax
from jax.experimental import pallas as pl
from jax.experimental.pallas import tpu as pltpu

F32 = jnp.float32
BF16 = jnp.bfloat16
HIGHEST = lax.Precision.HIGHEST

D_MODEL = 1024
D_FF = 4 * D_MODEL
N_MOD = 6
EPS = 1e-6
GRID_W = 64

S5_WIDTH = 256
S5_GROUP = 16
S5_GROUPS = 16
S5_STATE = 64
S5_N = S5_GROUPS * S5_STATE

HEAD = 64
HEADS = 6
WIDTH = HEADS * HEAD
GDN_CONV = 5
GDN_TAIL = 128
GDN_COLS_PACKED = 4 * WIDTH + GDN_TAIL
LORA = 64
GATE_LORA = 128
RWKV_COLS = 3 * WIDTH + 2 * LORA + 2 * LORA + GATE_LORA
RWKV_GN_EPS = 64e-5
CHUNK = 64
SUBLANES = 8
BLOCK_CHUNKS = 2

ROW_TILE = 256


def _dot(a, b, dims=((1,), (0,))):
    return lax.dot_general(a.astype(BF16), b.astype(BF16), (dims, ((), ())),
                           preferred_element_type=F32)


def _dot_f32(a, b, dims=((1,), (0,))):
    return lax.dot_general(a, b, (dims, ((), ())), precision=HIGHEST,
                           preferred_element_type=F32)


_NT = ((1,), (1,))
_TN = ((0,), (0,))


def _sigmoid(x):
    return 1.0 / (1.0 + jnp.exp(-x))


def _silu(x):
    return x * _sigmoid(x)


def _softplus(x):
    return jnp.maximum(x, 0.0) + jnp.log1p(jnp.exp(-jnp.abs(x)))


def _bmm(a, b, ta=False, tb=False):
    dims = (((1 if ta else 2,), (2 if tb else 1,)), ((0,), (0,)))
    return lax.dot_general(a.astype(BF16), b.astype(BF16), dims, preferred_element_type=F32)


def _tri_mask(shape, rev, strict):
    r = lax.broadcasted_iota(jnp.int32, shape, len(shape) - 2)
    c = lax.broadcasted_iota(jnp.int32, shape, len(shape) - 1)
    if rev:
        return (c > r) if strict else (c >= r)
    return (c < r) if strict else (c <= r)


def _chunk_cumsum_matrix(n, rev):
    r = lax.broadcasted_iota(jnp.int32, (n, n), 0)
    c = lax.broadcasted_iota(jnp.int32, (n, n), 1)
    shift = int(math.log2(CHUNK))
    same = (r >> shift) == (c >> shift)
    return jnp.where(same & ((c >= r) if rev else (c <= r)), 1.0, 0.0).astype(F32)


def _inv_unit_tri(a):
    n = a.shape[-1]
    r = lax.broadcasted_iota(jnp.int32, a.shape, 1)
    c = lax.broadcasted_iota(jnp.int32, a.shape, 2)
    x = -a
    inv = jnp.where(r == c, 1.0, 0.0).astype(F32) + x
    p = x
    for _ in range(int(math.log2(n)) - 1):
        p = _bmm(p, p)
        inv = inv + _bmm(inv, p)
    return inv


def _split_heads(x):
    nc = x.shape[0] // CHUNK
    return jnp.stack([x[ci * CHUNK:(ci + 1) * CHUNK, h * HEAD:(h + 1) * HEAD]
                      for ci in range(nc) for h in range(HEADS)], axis=0)


def _split_cols(x, first):
    nc = x.shape[0] // CHUNK
    return jnp.stack([x[ci * CHUNK:(ci + 1) * CHUNK, first + h:first + h + 1]
                      for ci in range(nc) for h in range(HEADS)], axis=0)


def _chunk_edge_rows(x, rev):
    nc = x.shape[0] // CHUNK
    edge = 0 if rev else CHUNK - 1
    return jnp.concatenate(
        [jnp.broadcast_to(x[ci * CHUNK + edge:ci * CHUNK + edge + 1], (CHUNK, x.shape[1]))
         for ci in range(nc)], axis=0)


def _head_sums(x):
    return [jnp.sum(x[:, h * HEAD:(h + 1) * HEAD], axis=-1, keepdims=True) for h in range(HEADS)]


def _head_bcast(cols, shape):
    hid = lax.broadcasted_iota(jnp.int32, shape, 1) >> int(math.log2(HEAD))
    out = jnp.broadcast_to(cols[HEADS - 1], shape)
    for h in range(HEADS - 2, -1, -1):
        out = jnp.where(hid == h, cols[h], out)
    return out


def _shift_rows(x, halo, offset):
    n = x.shape[0]
    row = lax.broadcasted_iota(jnp.int32, x.shape, 0)
    if offset < 0:
        k = -offset
        return jnp.where(row < k, pltpu.roll(halo, k, 0), pltpu.roll(x, k, 0))
    k = n - offset
    return jnp.where(row >= k, pltpu.roll(halo, k, 0), pltpu.roll(x, k, 0))


def _chunk_index(j, n_chunks, rev):
    return (n_chunks - 1 - j) if rev else j


def _cparams(sem):
    return pltpu.CompilerParams(dimension_semantics=sem)


def _mods_kernel(c_ref, w_ref, b_ref, o_ref):
    o_ref[0] = _dot_f32(_silu(c_ref[...]), w_ref[0]) + b_ref[0]


def _mods(cvec, mod_w, mod_b):
    depth = mod_w.shape[0]
    rows = cvec.shape[0]
    n_tile = 1536
    return pl.pallas_call(
        _mods_kernel,
        grid=(depth, N_MOD * D_MODEL // n_tile),
        in_specs=[
            pl.BlockSpec((rows, D_MODEL), lambda l, n: (0, 0)),
            pl.BlockSpec((1, D_MODEL, n_tile), lambda l, n: (l, 0, n)),
            pl.BlockSpec((1, 1, n_tile), lambda l, n: (l, 0, n)),
        ],
        out_specs=pl.BlockSpec((1, rows, n_tile), lambda l, n: (l, 0, n)),
        out_shape=jax.ShapeDtypeStruct((depth, rows, N_MOD * D_MODEL), F32),
        compiler_params=_cparams(("arbitrary", "arbitrary")),
        name="mods",
    )(cvec, mod_w, mod_b.reshape(depth, 1, N_MOD * D_MODEL))


def _inproj_kernel(x_ref, g_ref, sc_ref, sh_ref, w_ref, oa_ref, ob_ref, oc_ref):
    x = x_ref[0]
    xn = x * lax.rsqrt(jnp.mean(x * x, axis=-1, keepdims=True) + EPS) * g_ref[...]
    xm = xn * (1.0 + sc_ref[0]) + sh_ref[0]
    p = _dot(xm, w_ref[...])
    oa_ref[0] = p[:, :S5_WIDTH]
    ob_ref[0] = p[:, S5_WIDTH:S5_WIDTH + GDN_COLS_PACKED]
    oc_ref[0] = p[:, S5_WIDTH + GDN_COLS_PACKED:]


def _inproj(x, gain, sc, sh, w_packed):
    b, l, _ = x.shape
    tm = min(ROW_TILE, l)
    cols = w_packed.shape[1]
    vec = pl.BlockSpec((1, 1, D_MODEL), lambda i, t: (i, 0, 0))
    return pl.pallas_call(
        _inproj_kernel,
        grid=(b, l // tm),
        in_specs=[
            pl.BlockSpec((1, tm, D_MODEL), lambda i, t: (i, t, 0)),
            pl.BlockSpec((1, D_MODEL), lambda i, t: (0, 0)),
            vec, vec,
            pl.BlockSpec((D_MODEL, cols), lambda i, t: (0, 0)),
        ],
        out_specs=[
            pl.BlockSpec((1, tm, S5_WIDTH), lambda i, t: (i, t, 0)),
            pl.BlockSpec((1, tm, GDN_COLS_PACKED), lambda i, t: (i, t, 0)),
            pl.BlockSpec((1, tm, RWKV_COLS), lambda i, t: (i, t, 0)),
        ],
        out_shape=[
            jax.ShapeDtypeStruct((b, l, S5_WIDTH), F32),
            jax.ShapeDtypeStruct((b, l, GDN_COLS_PACKED), F32),
            jax.ShapeDtypeStruct((b, l, RWKV_COLS), F32),
        ],
        compiler_params=_cparams(("parallel", "arbitrary")),
        name="inproj",
    )(x, gain.reshape(1, D_MODEL), sc, sh, w_packed)


def _outproj_kernel(x_ref, ya_ref, yb_ref, yc_ref, woa_ref, wob_ref, woc_ref, gt1_ref,
                    g_ref, sc_ref, sh_ref, gt2_ref, w1_ref, w2_ref, gf_ref, o_ref, *, final):
    mix = (_dot(ya_ref[0], woa_ref[...]) + _dot(yb_ref[0], wob_ref[...])
           + _dot(yc_ref[0], woc_ref[...]))
    x1 = x_ref[0] + gt1_ref[0] * mix
    h = x1 * lax.rsqrt(jnp.mean(x1 * x1, axis=-1, keepdims=True) + EPS) * g_ref[...]
    h = h * (1.0 + sc_ref[0]) + sh_ref[0]
    a = jnp.maximum(_dot(h, w1_ref[...]), 0.0)
    x2 = x1 + gt2_ref[0] * _dot(a * a, w2_ref[...])
    if final:
        x2 = x2 * lax.rsqrt(jnp.mean(x2 * x2, axis=-1, keepdims=True) + EPS) * gf_ref[...]
    o_ref[0] = x2


def _outproj(x, ya, yb, yc, wo_parts, gt1, gain, sc, sh, gt2, w1, w2, gain_final, final):
    b, l, _ = x.shape
    tm = min(ROW_TILE, l)
    vec = pl.BlockSpec((1, 1, D_MODEL), lambda i, t: (i, 0, 0))
    row = pl.BlockSpec((1, D_MODEL), lambda i, t: (0, 0))

    def tok(width):
        return pl.BlockSpec((1, tm, width), lambda i, t: (i, t, 0))

    def full(arr):
        return pl.BlockSpec(arr.shape, lambda i, t: (0, 0))

    woa, wob, woc = wo_parts
    return pl.pallas_call(
        functools.partial(_outproj_kernel, final=final),
        grid=(b, l // tm),
        in_specs=[tok(D_MODEL), tok(S5_WIDTH), tok(WIDTH), tok(WIDTH),
                  full(woa), full(wob), full(woc), vec, row, vec, vec, vec,
                  full(w1), full(w2), row],
        out_specs=tok(D_MODEL),
        out_shape=jax.ShapeDtypeStruct((b, l, D_MODEL), F32),
        compiler_params=_cparams(("parallel", "arbitrary")),
        name="outproj_mlp",
    )(x, ya, yb, yc, woa, wob, woc, gt1, gain.reshape(1, D_MODEL), sc, sh, gt2, w1, w2,
      gain_final.reshape(1, D_MODEL))


def _cmul(a, b):
    return a[0] * b[0] - a[1] * b[1], a[0] * b[1] + a[1] * b[0]


def _s5_kernel(*refs, rev, readout, n_chunks):
    (u_ref, are_ref, aim_ref, ldt_ref, bre_ref, bim_ref, cre_ref, cim_ref,
     h0re_ref, h0im_ref) = refs[:10]
    if readout:
        d_ref, gw_ref, gb_ref, yf_ref = refs[10:14]
        y_ref, hre_ref, him_ref = refs[14:17]
        wre, wim, lvl, pw, carry = refs[17:]
    else:
        y_ref, hre_ref, him_ref = refs[10:13]
        wre, wim, lvl, pw, carry = refs[13:]
    t = u_ref.shape[1]
    j = pl.program_id(1)
    tile = (SUBLANES, S5_N)

    @pl.when(j == 0)
    def _discretise():
        dt = jnp.exp(ldt_ref[...])
        a_re = are_ref[...]
        a_im = aim_ref[...]
        mag = jnp.exp(a_re * dt)
        ab = (mag * jnp.cos(a_im * dt), mag * jnp.sin(a_im * dt))
        den = a_re * a_re + a_im * a_im
        xr = ab[0] - 1.0
        cr = (xr * a_re + ab[1] * a_im) / den
        ci = (ab[1] * a_re - xr * a_im) / den
        wre[...] = (bre_ref[...] * cr - bim_ref[...] * ci).astype(BF16)
        wim[...] = (bre_ref[...] * ci + bim_ref[...] * cr).astype(BF16)
        powers = [ab]
        for _ in range(SUBLANES - 1):
            powers.append(_cmul(powers[-1], ab))
        row = lax.broadcasted_iota(jnp.int32, tile, 0)
        for li, s in enumerate((1, 2, 4)):
            keep = (row < SUBLANES - s) if rev else (row >= s)
            lvl[2 * li] = jnp.where(keep, powers[s - 1][0], 0.0)
            lvl[2 * li + 1] = jnp.where(keep, powers[s - 1][1], 0.0)
        pr = jnp.zeros(tile, F32)
        pi = jnp.zeros(tile, F32)
        for r in range(SUBLANES):
            k = (SUBLANES - 1 - r) if rev else r
            pr = jnp.where(row == r, powers[k][0], pr)
            pi = jnp.where(row == r, powers[k][1], pi)
        pw[0] = pr
        pw[1] = pi
        carry[0:1, :] = h0re_ref[0]
        carry[1:2, :] = h0im_ref[0]

    u = u_ref[0]
    bur = _dot(u, wre[...])
    bui = _dot(u, wim[...])
    c_re = carry[0:1, :]
    c_im = carry[1:2, :]
    n_tiles = t // SUBLANES
    h_re = [None] * n_tiles
    h_im = [None] * n_tiles
    for i in (range(n_tiles - 1, -1, -1) if rev else range(n_tiles)):
        tr = bur[i * SUBLANES:(i + 1) * SUBLANES]
        ti = bui[i * SUBLANES:(i + 1) * SUBLANES]
        for li, s in enumerate((1, 2, 4)):
            sh = (SUBLANES - s) if rev else s
            sr = pltpu.roll(tr, sh, 0)
            si = pltpu.roll(ti, sh, 0)
            lr = lvl[2 * li]
            lim = lvl[2 * li + 1]
            tr, ti = tr + lr * sr - lim * si, ti + lr * si + lim * sr
        cbr = jnp.broadcast_to(c_re, tile)
        cbi = jnp.broadcast_to(c_im, tile)
        hr = tr + pw[0] * cbr - pw[1] * cbi
        hi = ti + pw[0] * cbi + pw[1] * cbr
        edge = 0 if rev else SUBLANES - 1
        c_re = hr[edge:edge + 1]
        c_im = hi[edge:edge + 1]
        h_re[i] = hr
        h_im[i] = hi
    carry[0:1, :] = c_re
    carry[1:2, :] = c_im
    y = (_dot(jnp.concatenate(h_re, axis=0), cre_ref[...])
         - _dot(jnp.concatenate(h_im, axis=0), cim_ref[...]))
    if readout:
        y = y + yf_ref[0] + d_ref[...] * u
        z = 0.5 * y * (1.0 + jnp.tanh(math.sqrt(2.0 / math.pi) * (y + 0.044715 * y * y * y)))
        y = z * _sigmoid(_dot(z, gw_ref[...]) + gb_ref[...])
    y_ref[0] = y

    @pl.when(j == n_chunks - 1)
    def _emit_state():
        hre_ref[0] = c_re
        him_ref[0] = c_im


def _s5_call(u, prm, d, h0, rev, readout, yf=None):
    b, l, _ = u.shape
    t = CHUNK
    n_chunks = l // t
    tok = pl.BlockSpec((1, t, S5_WIDTH), lambda i, j: (i, _chunk_index(j, n_chunks, rev), 0))
    row_n = pl.BlockSpec((1, S5_N), lambda i, j: (0, 0))
    st = pl.BlockSpec((1, 1, S5_N), lambda i, j: (i, 0, 0))

    def full(arr):
        return pl.BlockSpec(arr.shape, lambda i, j: (0,) * arr.ndim)

    args = [u, prm["a_re"][d], prm["a_im"][d], prm["log_dt"][d], prm["b_re"], prm["b_im"],
            prm["c_re"], prm["c_im"], h0[0], h0[1]]
    specs = [tok, row_n, row_n, row_n, full(prm["b_re"]), full(prm["b_im"]),
             full(prm["c_re"]), full(prm["c_im"]), st, st]
    if readout:
        args += [prm["d"], prm["glu_w"], prm["glu_b"], yf]
        specs += [full(prm["d"]), full(prm["glu_w"]), full(prm["glu_b"]), tok]
    y, hre, him = pl.pallas_call(
        functools.partial(_s5_kernel, rev=rev, readout=readout, n_chunks=n_chunks),
        grid=(b, n_chunks),
        in_specs=specs,
        out_specs=[tok, st, st],
        out_shape=[jax.ShapeDtypeStruct((b, l, S5_WIDTH), F32),
                   jax.ShapeDtypeStruct((b, 1, S5_N), F32),
                   jax.ShapeDtypeStruct((b, 1, S5_N), F32)],
        scratch_shapes=[pltpu.VMEM((S5_WIDTH, S5_N), BF16), pltpu.VMEM((S5_WIDTH, S5_N), BF16),
                        pltpu.VMEM((6, SUBLANES, S5_N), F32), pltpu.VMEM((2, SUBLANES, S5_N), F32),
                        pltpu.VMEM((2, S5_N), F32)],
        compiler_params=_cparams(("parallel", "arbitrary")),
        name="s5_bwd" if rev else "s5_fwd",
    )(*args)
    return y, (hre, him)


def _s5_mixer(u_ctx, u_lat, prm):
    b = u_lat.shape[0]
    zero = (jnp.zeros((b, 1, S5_N), F32), jnp.zeros((b, 1, S5_N), F32))
    yc_f, hc_f = _s5_call(u_ctx, prm, 0, zero, False, False)
    y_ctx, hc_b = _s5_call(u_ctx, prm, 1, zero, True, True, yc_f)
    yl_f, _ = _s5_call(u_lat, prm, 0, hc_f, False, False)
    y_lat, _ = _s5_call(u_lat, prm, 1, hc_b, True, True, yl_f)
    return y_ctx, y_lat


def _block_order(nc, rev):
    return range(nc - 1, -1, -1) if rev else range(nc)


def _store_heads(buf, ci, o):
    for h in range(HEADS):
        buf[ci * CHUNK:(ci + 1) * CHUNK, h * HEAD:(h + 1) * HEAD] = o[h]


def _gdn_kernel(*refs, rev, readout, n_blocks, d):
    pp_ref, pc_ref, pn_ref, cw_ref, alog_ref, dtb_ref, s0_ref = refs[:7]
    if readout:
        nw_ref, of_ref = refs[7:9]
        y_ref, so_ref = refs[9:11]
        state, obuf = refs[11:]
    else:
        y_ref, so_ref = refs[7:9]
        state, obuf = refs[9:]
    bt = pc_ref.shape[1]
    nc = bt // CHUNK
    j = pl.program_id(1)
    c = _chunk_index(j, n_blocks, rev)

    @pl.when(j == 0)
    def _load_state():
        state[...] = s0_ref[0]

    x = pc_ref[0]
    qkv = x[:, :3 * WIDTH]
    prev = jnp.where(c > 0, pp_ref[0][:, :3 * WIDTH], 0.0)
    nxt = jnp.where(c < n_blocks - 1, pn_ref[0][:, :3 * WIDTH], 0.0)
    half = GDN_CONV // 2
    acc = cw_ref[half:half + 1, :] * qkv
    for off in range(-half, half + 1):
        if off != 0:
            acc = acc + cw_ref[half + off:half + off + 1, :] * _shift_rows(
                qkv, prev if off < 0 else nxt, off)
    act = _silu(acc)
    q = act[:, :WIDTH]
    k = act[:, WIDTH:2 * WIDTH]
    v = act[:, 2 * WIDTH:]
    q = q * _head_bcast([lax.rsqrt(s + EPS) * HEAD ** -0.5 for s in _head_sums(q * q)], q.shape)
    k = k * _head_bcast([lax.rsqrt(s + EPS) for s in _head_sums(k * k)], k.shape)
    tail = x[:, 4 * WIDTH:]
    beta_all = _sigmoid(tail)
    g_all = -jnp.exp(alog_ref[...]) * _softplus(tail + dtb_ref[...])
    gcs = _dot_f32(_chunk_cumsum_matrix(bt, rev), g_all)
    gcs_t = gcs.T
    first_g = 2 * HEADS + d * HEADS
    gc = _split_cols(gcs, first_g)
    gr = jnp.stack([gcs_t[first_g + h:first_g + h + 1, ci * CHUNK:(ci + 1) * CHUNK]
                    for ci in range(nc) for h in range(HEADS)], axis=0)
    beta = _split_cols(beta_all, d * HEADS)
    qs = _split_heads(q)
    ks = _split_heads(k)
    vs = _split_heads(v)
    sq = (nc * HEADS, CHUNK, CHUNK)
    incl = _tri_mask(sq, rev, False)
    strict = _tri_mask(sq, rev, True)
    decay = jnp.where(incl, jnp.exp(jnp.minimum(gc - gr, 0.0)), 0.0)
    kb = ks * beta
    big = _bmm(jnp.concatenate([kb, qs], axis=1), ks, tb=True)
    m = jnp.where(strict, big[:, :CHUNK] * decay, 0.0)
    attn = big[:, CHUNK:] * decay
    tinv = _inv_unit_tri(m)
    eg = jnp.exp(gc)
    uw = _bmm(tinv, jnp.concatenate([vs * beta, kb * eg], axis=2))
    edge = 0 if rev else CHUNK - 1
    g_last = gc[:, edge:edge + 1]
    kd = ks * jnp.exp(g_last - gc)
    au = _bmm(attn, uw)
    q_eff = qs * eg - au[:, :, HEAD:]
    y_in = au[:, :, :HEAD]
    kdu = _bmm(kd, uw, ta=True)
    dec_last = jnp.exp(g_last)
    s = state[...]
    for ci in _block_order(nc, rev):
        sl = slice(ci * HEADS, (ci + 1) * HEADS)
        _store_heads(obuf, ci, _bmm(q_eff[sl], s) + y_in[sl])
        s = s * dec_last[sl] - _bmm(kdu[sl, :, HEAD:], s) + kdu[sl, :, :HEAD]
    state[...] = s
    o = obuf[...]
    if readout:
        o = o + of_ref[0]
        z = x[:, 3 * WIDTH:4 * WIDTH]
        scale = _head_bcast([lax.rsqrt(s_ * (1.0 / HEAD) + EPS) for s_ in _head_sums(o * o)],
                            o.shape)
        o = o * scale * nw_ref[...] * _silu(z)
    y_ref[0] = o

    @pl.when(j == n_blocks - 1)
    def _emit_state():
        so_ref[0] = state[...]


def _block_rows(l):
    return min(BLOCK_CHUNKS * CHUNK, l)


def _halo_specs(width, bt, n_blocks, rev):
    def cur(i, j):
        return (i, _chunk_index(j, n_blocks, rev), 0)

    def prev(i, j):
        return (i, jnp.maximum(_chunk_index(j, n_blocks, rev) - 1, 0), 0)

    def nxt(i, j):
        return (i, jnp.minimum(_chunk_index(j, n_blocks, rev) + 1, n_blocks - 1), 0)

    blk = (1, bt, width)
    return [pl.BlockSpec(blk, prev), pl.BlockSpec(blk, cur), pl.BlockSpec(blk, nxt)]


def _gdn_call(p, prm, d, s0, rev, readout, of=None):
    b, l, _ = p.shape
    bt = _block_rows(l)
    n_blocks = l // bt
    tok = pl.BlockSpec((1, bt, WIDTH), lambda i, j: (i, _chunk_index(j, n_blocks, rev), 0))
    st = pl.BlockSpec((1, HEADS, HEAD, HEAD), lambda i, j: (i, 0, 0, 0))

    def full(arr):
        return pl.BlockSpec(arr.shape, lambda i, j: (0,) * arr.ndim)

    args = [p, p, p, prm["conv"], prm["alog"][d], prm["dtb"][d], s0]
    specs = _halo_specs(GDN_COLS_PACKED, bt, n_blocks, rev) + [
        full(prm["conv"]), full(prm["alog"][d]), full(prm["dtb"][d]), st]
    if readout:
        args += [prm["norm"], of]
        specs += [full(prm["norm"]), tok]
    return pl.pallas_call(
        functools.partial(_gdn_kernel, rev=rev, readout=readout, n_blocks=n_blocks, d=d),
        grid=(b, n_blocks),
        in_specs=specs,
        out_specs=[tok, st],
        out_shape=[jax.ShapeDtypeStruct((b, l, WIDTH), F32),
                   jax.ShapeDtypeStruct((b, HEADS, HEAD, HEAD), F32)],
        scratch_shapes=[pltpu.VMEM((HEADS, HEAD, HEAD), F32), pltpu.VMEM((bt, WIDTH), F32)],
        compiler_params=_cparams(("parallel", "arbitrary")),
        name="gdn_bwd" if rev else "gdn_fwd",
    )(*args)


def _gdn_mixer(p_ctx, p_lat, prm):
    b = p_lat.shape[0]
    zero = jnp.zeros((b, HEADS, HEAD, HEAD), F32)
    oc_f, sc_f = _gdn_call(p_ctx, prm, 0, zero, False, False)
    y_ctx, sc_b = _gdn_call(p_ctx, prm, 1, zero, True, True, oc_f)
    ol_f, _ = _gdn_call(p_lat, prm, 0, sc_f, False, False)
    y_lat, _ = _gdn_call(p_lat, prm, 1, sc_b, True, True, ol_f)
    return y_ctx, y_lat


def _rwkv_kernel(*refs, rev, readout, grid_shift, n_blocks):
    (pp_ref, pc_ref, pn_ref, mu_ref, w0_ref, wup_ref, a0_ref, aup_ref, kk_ref, ka_ref, rk_ref,
     s0_ref) = refs[:12]
    if readout:
        gup_ref, lnw_ref, lnb_ref, yf_ref, bf_ref = refs[12:17]
        y_ref, so_ref = refs[17:19]
        state, obuf = refs[19:]
    else:
        y_ref, bon_ref, so_ref = refs[12:15]
        state, obuf = refs[15:]
    bt = pc_ref.shape[1]
    nc = bt // CHUNK
    j = pl.program_id(1)
    c = _chunk_index(j, n_blocks, rev)

    @pl.when(j == 0)
    def _load_state():
        state[...] = s0_ref[0]

    x = pc_ref[0]
    prev = jnp.where(c > 0, pp_ref[0], 0.0)
    nxt = jnp.where(c < n_blocks - 1, pn_ref[0], 0.0)
    lane = lax.broadcasted_iota(jnp.int32, x.shape, 1)
    if grid_shift:
        col = lax.broadcasted_iota(jnp.int32, x.shape, 0) & (GRID_W - 1)
        left = jnp.where(col == 0, 0.0, pltpu.roll(x, 1, 0))
        right = jnp.where(col == GRID_W - 1, 0.0, pltpu.roll(x, bt - 1, 0))
        up = _shift_rows(x, prev, -GRID_W)
        down = _shift_rows(x, nxt, GRID_W)
        cm = lane & 3
        shifted = jnp.where(cm == 0, left, jnp.where(cm == 1, right, jnp.where(cm == 2, up, down)))
    else:
        shifted = jnp.where((lane & 1) == 0, _shift_rows(x, prev, -1), _shift_rows(x, nxt, 1))
    pm = x + (shifted - x) * mu_ref[...]
    r = pm[:, :WIDTH]
    k = pm[:, WIDTH:2 * WIDTH]
    v = pm[:, 2 * WIDTH:3 * WIDTH]
    o1 = 3 * WIDTH
    w_dn = pm[:, o1:o1 + 2 * LORA]
    a_dn = pm[:, o1 + 2 * LORA:o1 + 4 * LORA]
    g_dn = pm[:, o1 + 4 * LORA:]
    w_log = -_softplus(-(w0_ref[...] + _dot(jnp.tanh(w_dn), wup_ref[...]))) - 0.5
    lw = -jnp.exp(w_log)
    iclr = _sigmoid(a0_ref[...] + _dot(a_dn, aup_ref[...]))
    kkr = k * kk_ref[...]
    kk = kkr * _head_bcast([lax.rsqrt(s + EPS) for s in _head_sums(kkr * kkr)], kkr.shape)
    k_dir = k * (1.0 + (iclr - 1.0) * ka_ref[...])
    b_dir = kk * iclr
    bonus_cols = _head_sums(r * k_dir * rk_ref[...])

    g_inc = _dot_f32(_chunk_cumsum_matrix(bt, rev), lw)
    g_end = _chunk_edge_rows(g_inc, rev)
    e_neg = jnp.exp(-g_inc)
    e_end = jnp.exp(g_end - g_inc)
    kap_h = _split_heads(kk * jnp.exp(g_inc - lw))
    r_h = _split_heads(r * jnp.exp(g_inc))
    b_h = _split_heads(b_dir * e_neg)
    k_h = _split_heads(k_dir * e_neg)
    b_eh = _split_heads(b_dir * e_end)
    k_eh = _split_heads(k_dir * e_end)
    v_h = _split_heads(v)
    dec_end = _split_heads(jnp.exp(g_end))[:, :1, :]
    sq = (nc * HEADS, CHUNK, CHUNK)
    incl = _tri_mask(sq, rev, False)
    strict = _tri_mask(sq, rev, True)
    big = _bmm(jnp.concatenate([kap_h, r_h], axis=1), jnp.concatenate([b_h, k_h], axis=1),
               tb=True)
    a_b = jnp.where(strict, big[:, :CHUNK, :CHUNK], 0.0)
    a_k = jnp.where(strict, big[:, :CHUNK, CHUNK:], 0.0)
    r_b = jnp.where(incl, big[:, CHUNK:, :CHUNK], 0.0)
    r_k = jnp.where(incl, big[:, CHUNK:, CHUNK:], 0.0)
    tinv = _inv_unit_tri(a_b)
    z = _bmm(tinv, jnp.concatenate([kap_h, _bmm(a_k, v_h)], axis=2))
    rz = _bmm(r_b, z)
    q_eff = r_h - rz[:, :, :HEAD]
    y_in = _bmm(r_k, v_h) - rz[:, :, HEAD:]
    ztb = _bmm(z, b_eh, ta=True)
    pt = ztb[:, :HEAD]
    xt = _bmm(v_h, k_eh, ta=True) - ztb[:, HEAD:]
    s = state[...]
    for ci in _block_order(nc, rev):
        sl = slice(ci * HEADS, (ci + 1) * HEADS)
        _store_heads(obuf, ci, _bmm(q_eff[sl], s, tb=True) + y_in[sl])
        s = s * dec_end[sl] - _bmm(s, pt[sl]) + xt[sl]
    state[...] = s
    y = obuf[...]
    bonus = _head_bcast(bonus_cols, v.shape) * v
    if readout:
        y = y + yf_ref[0]
        mean = _head_bcast([s_ * (1.0 / HEAD) for s_ in _head_sums(y)], y.shape)
        yc = y - mean
        var = _head_bcast([s_ * (1.0 / HEAD) for s_ in _head_sums(yc * yc)], y.shape)
        yn = yc * lax.rsqrt(var + RWKV_GN_EPS) * lnw_ref[...] + lnb_ref[...]
        gate = _dot(_sigmoid(g_dn), gup_ref[...])
        y_ref[0] = (yn + bonus + bf_ref[0]) * gate
    else:
        y_ref[0] = y
        bon_ref[0] = bonus

    @pl.when(j == n_blocks - 1)
    def _emit_state():
        so_ref[0] = state[...]


def _rwkv_call(p, prm, d, s0, rev, readout, grid_shift, yf=None, bf=None):
    b, l, _ = p.shape
    bt = _block_rows(l)
    n_blocks = l // bt
    tok = pl.BlockSpec((1, bt, WIDTH), lambda i, j: (i, _chunk_index(j, n_blocks, rev), 0))
    st = pl.BlockSpec((1, HEADS, HEAD, HEAD), lambda i, j: (i, 0, 0, 0))

    def full(arr):
        return pl.BlockSpec(arr.shape, lambda i, j: (0,) * arr.ndim)

    small = [prm["mu"], prm["w0"][d], prm["w_up"][d], prm["a0"][d], prm["a_up"][d],
             prm["k_k"], prm["k_a"], prm["r_k"]]
    args = [p, p, p] + small + [s0]
    specs = _halo_specs(RWKV_COLS, bt, n_blocks, rev) + [full(a) for a in small] + [st]
    tok_shape = jax.ShapeDtypeStruct((b, l, WIDTH), F32)
    st_shape = jax.ShapeDtypeStruct((b, HEADS, HEAD, HEAD), F32)
    if readout:
        extra = [prm["g_up"], prm["ln_w"], prm["ln_b"]]
        args += extra + [yf, bf]
        specs += [full(a) for a in extra] + [tok, tok]
        out_specs, out_shape = [tok, st], [tok_shape, st_shape]
    else:
        out_specs, out_shape = [tok, tok, st], [tok_shape, tok_shape, st_shape]
    return pl.pallas_call(
        functools.partial(_rwkv_kernel, rev=rev, readout=readout, grid_shift=grid_shift,
                          n_blocks=n_blocks),
        grid=(b, n_blocks),
        in_specs=specs,
        out_specs=out_specs,
        out_shape=out_shape,
        scratch_shapes=[pltpu.VMEM((HEADS, HEAD, HEAD), F32), pltpu.VMEM((bt, WIDTH), F32)],
        compiler_params=_cparams(("parallel", "arbitrary")),
        name="rwkv_bwd" if rev else "rwkv_fwd",
    )(*args)


def _rwkv_mixer(p_ctx, p_lat, prm):
    b = p_lat.shape[0]
    zero = jnp.zeros((b, HEADS, HEAD, HEAD), F32)
    yc_f, bc_f, sc_f = _rwkv_call(p_ctx, prm, 0, zero, False, False, False)
    y_ctx, sc_b = _rwkv_call(p_ctx, prm, 1, zero, True, True, False, yc_f, bc_f)
    yl_f, bl_f, _ = _rwkv_call(p_lat, prm, 0, sc_f, False, False, True)
    y_lat, _ = _rwkv_call(p_lat, prm, 1, sc_b, True, True, True, yl_f, bl_f)
    return y_ctx, y_lat


def _pack_w_in(w):
    o_gdn = S5_WIDTH
    o_tail = o_gdn + 4 * WIDTH
    o_rwkv = o_tail + 4 * HEADS
    pad = jnp.zeros((w.shape[0], GDN_TAIL - 4 * HEADS), w.dtype)
    return jnp.concatenate([w[:, :o_tail], w[:, o_tail:o_rwkv], pad, w[:, o_rwkv:]], axis=1)


def _block_diag(blocks):
    g, r, c = blocks.shape
    eye = jnp.eye(g, dtype=blocks.dtype)
    return (eye[:, None, :, None] * blocks[:, :, None, :]).reshape(g * r, g * c)


def _s5_params(layer, b_re, b_im, c_re, c_im, d, a_re, a_im, log_dt, glu_w, glu_b):
    row = lambda v: v.reshape(2, 1, S5_N)
    return {
        "a_re": row(a_re[layer]), "a_im": row(a_im[layer]),
        "log_dt": row(jnp.repeat(log_dt[layer], S5_STATE, axis=-1)),
        "b_re": _block_diag(jnp.swapaxes(b_re[layer], 1, 2)),
        "b_im": _block_diag(jnp.swapaxes(b_im[layer], 1, 2)),
        "c_re": _block_diag(jnp.swapaxes(c_re[layer], 1, 2)),
        "c_im": _block_diag(jnp.swapaxes(c_im[layer], 1, 2)),
        "d": d[layer].reshape(1, S5_WIDTH), "glu_w": glu_w[layer],
        "glu_b": glu_b[layer].reshape(1, S5_WIDTH),
    }


def _gdn_params(layer, conv, a_log, dt_bias, norm):
    def tail_row(v):
        rows = []
        for d in range(2):
            z = jnp.zeros((GDN_TAIL,), F32)
            rows.append(z.at[2 * HEADS + d * HEADS:2 * HEADS + (d + 1) * HEADS].set(v[d])
                        .reshape(1, GDN_TAIL))
        return rows

    return {"conv": conv[layer], "alog": tail_row(a_log[layer]), "dtb": tail_row(dt_bias[layer]),
            "norm": jnp.tile(norm[layer], HEADS).reshape(1, WIDTH)}


def _rwkv_params(layer, mu, w0, w_up, a0, a_up, g_up, k_k, k_a, r_k, ln_w, ln_b):
    def lora_pad(w):
        z = jnp.zeros((LORA, WIDTH), F32)
        return [jnp.concatenate([w[0], z], axis=0), jnp.concatenate([z, w[1]], axis=0)]

    row = lambda v: v.reshape(1, -1)
    return {
        "mu": row(mu[layer]), "w0": [row(w0[layer, 0]), row(w0[layer, 1])],
        "w_up": lora_pad(w_up[layer]), "a0": [row(a0[layer, 0]), row(a0[layer, 1])],
        "a_up": lora_pad(a_up[layer]), "g_up": g_up[layer], "k_k": row(k_k[layer]),
        "k_a": row(k_a[layer]), "r_k": row(r_k[layer]), "ln_w": row(ln_w[layer]),
        "ln_b": row(ln_b[layer]),
    }


def kernel(x, c, ctx, c_ctx, mod_w, mod_b, norm_mix, norm_mlp, norm_final, w_in, w_out, s5_b_re, s5_b_im, s5_c_re, s5_c_im, s5_d, s5_a_re, s5_a_im, s5_log_dt, s5_glu_w, s5_glu_b, gdn_conv, gdn_a_log, gdn_dt_bias, gdn_norm, rwkv_mu, rwkv_w0, rwkv_w_up, rwkv_a0, rwkv_a_up, rwkv_g_up, rwkv_k_k, rwkv_k_a, rwkv_r_k, rwkv_ln_w, rwkv_ln_b, mlp_w1, mlp_w2):
    depth = mod_w.shape[0]
    batch = x.shape[0]
    x = x.astype(F32)
    ctx = ctx.astype(F32)
    rows = SUBLANES * ((batch + 1 + SUBLANES - 1) // SUBLANES)
    cvec = jnp.zeros((rows, D_MODEL), F32).at[:batch].set(c.astype(F32)).at[batch].set(
        c_ctx.astype(F32))
    mods_all = _mods(cvec, mod_w, mod_b)

    for layer in range(depth):
        ctx_out = layer < depth - 1
        m = mods_all[layer].reshape(rows, N_MOD, 1, D_MODEL)
        lat = [m[:batch, i] for i in range(N_MOD)]
        cm = [jnp.broadcast_to(m[batch, i][None], (batch, 1, D_MODEL)) for i in range(N_MOD)]
        w_packed = _pack_w_in(w_in[layer]).astype(BF16)
        wo = w_out[layer].astype(BF16)
        wo_parts = (wo[:S5_WIDTH], wo[S5_WIDTH:S5_WIDTH + WIDTH], wo[S5_WIDTH + WIDTH:])
        w1 = mlp_w1[layer].astype(BF16)
        w2 = mlp_w2[layer].astype(BF16)

        pa_l, pb_l, pc_l = _inproj(x, norm_mix[layer], lat[1], lat[0], w_packed)
        pa_c, pb_c, pc_c = _inproj(ctx, norm_mix[layer], cm[1], cm[0], w_packed)

        ya_c, ya_l = _s5_mixer(pa_c, pa_l, _s5_params(
            layer, s5_b_re, s5_b_im, s5_c_re, s5_c_im, s5_d, s5_a_re, s5_a_im, s5_log_dt,
            s5_glu_w, s5_glu_b))
        yb_c, yb_l = _gdn_mixer(pb_c, pb_l, _gdn_params(
            layer, gdn_conv, gdn_a_log, gdn_dt_bias, gdn_norm))
        yc_c, yc_l = _rwkv_mixer(pc_c, pc_l, _rwkv_params(
            layer, rwkv_mu, rwkv_w0, rwkv_w_up, rwkv_a0, rwkv_a_up, rwkv_g_up, rwkv_k_k,
            rwkv_k_a, rwkv_r_k, rwkv_ln_w, rwkv_ln_b))

        x = _outproj(x, ya_l, yb_l, yc_l, wo_parts, lat[2], norm_mlp[layer], lat[4], lat[3],
                     lat[5], w1, w2, norm_final, final=not ctx_out)
        if ctx_out:
            ctx = _outproj(ctx, ya_c, yb_c, yc_c, wo_parts, cm[2], norm_mlp[layer], cm[4], cm[3],
                           cm[5], w1, w2, norm_final, final=False)
    return x
```

```python
import functools
import math

import jax
import jax.numpy as jnp
from jax import lax
from jax.experimental import pallas as pl
from jax.experimental.pallas import tpu as pltpu

F32 = jnp.float32
BF16 = jnp.bfloat16
HIGHEST = lax.Precision.HIGHEST

D_MODEL = 1024
D_FF = 4 * D_MODEL
N_MOD = 6
EPS = 1e-6
GRID_W = 64

S5_WIDTH = 256
S5_GROUP = 16
S5_GROUPS = 16
S5_STATE = 64
S5_N = S5_GROUPS * S5_STATE
S5_BLOCK = 256

HEAD = 64
HEADS = 6
WIDTH = HEADS * HEAD
GDN_CONV = 5
GDN_TAIL = 128
GDN_COLS_PACKED = 4 * WIDTH + GDN_TAIL
LORA = 64
GATE_LORA = 128
RWKV_COLS = 3 * WIDTH + 2 * LORA + 2 * LORA + GATE_LORA
RWKV_GN_EPS = 64e-5
CHUNK = 64
SUBLANES = 8
BLOCK_CHUNKS = 4

ROW_TILE = 512


def _dot(a, b, dims=((1,), (0,))):
    return lax.dot_general(a.astype(BF16), b.astype(BF16), (dims, ((), ())),
                           preferred_element_type=F32)


def _dot_f32(a, b, dims=((1,), (0,))):
    return lax.dot_general(a, b, (dims, ((), ())), precision=HIGHEST,
                           preferred_element_type=F32)


_NT = ((1,), (1,))
_TN = ((0,), (0,))


def _sigmoid(x):
    return 0.5 + 0.5 * jnp.tanh(0.5 * x)


def _silu(x):
    h = 0.5 * x
    return h + h * jnp.tanh(h)


def _softplus(x):
    return jnp.maximum(x, 0.0) + jnp.log1p(jnp.exp(-jnp.abs(x)))


def _bmm(a, b, ta=False, tb=False):
    dims = (((1 if ta else 2,), (2 if tb else 1,)), ((0,), (0,)))
    return lax.dot_general(a.astype(BF16), b.astype(BF16), dims, preferred_element_type=F32)


def _tri_mask(rev, strict):
    shape = (1, CHUNK, CHUNK)
    r = lax.broadcasted_iota(jnp.int32, shape, 1)
    c = lax.broadcasted_iota(jnp.int32, shape, 2)
    if rev:
        return (c > r) if strict else (c >= r)
    return (c < r) if strict else (c <= r)


def _chunk_cumsum_matrix(n, rev):
    r = lax.broadcasted_iota(jnp.int32, (n, n), 0)
    c = lax.broadcasted_iota(jnp.int32, (n, n), 1)
    shift = int(math.log2(CHUNK))
    same = (r >> shift) == (c >> shift)
    return jnp.where(same & ((c >= r) if rev else (c <= r)), 1.0, 0.0).astype(F32)


def _inv_unit_tri(a):
    n = a.shape[-1]
    r = lax.broadcasted_iota(jnp.int32, (1, n, n), 1)
    c = lax.broadcasted_iota(jnp.int32, (1, n, n), 2)
    x = -a
    inv = jnp.where(r == c, 1.0, 0.0).astype(F32) + x
    p = x
    for _ in range(int(math.log2(n)) - 1):
        p = _bmm(p, p)
        inv = inv + _bmm(inv, p)
    return inv


def _split_heads(x):
    nc = x.shape[0] // CHUNK
    return jnp.stack([x[ci * CHUNK:(ci + 1) * CHUNK, h * HEAD:(h + 1) * HEAD]
                      for ci in range(nc) for h in range(HEADS)], axis=0)


def _split_cols(x, first):
    nc = x.shape[0] // CHUNK
    return jnp.stack([x[ci * CHUNK:(ci + 1) * CHUNK, first + h:first + h + 1]
                      for ci in range(nc) for h in range(HEADS)], axis=0)


def _chunk_edge_rows(x, rev):
    nc = x.shape[0] // CHUNK
    edge = 0 if rev else CHUNK - 1
    return jnp.concatenate(
        [jnp.broadcast_to(x[ci * CHUNK + edge:ci * CHUNK + edge + 1], (CHUNK, x.shape[1]))
         for ci in range(nc)], axis=0)


def _head_sum(x, ones_ref):
    return _dot(x, ones_ref[...])


def _shift_rows(x, before, after, offset):
    n = x.shape[0]
    if offset < 0:
        h = before.shape[0]
        ext = jnp.concatenate([before, x], axis=0)
        if offset % SUBLANES:
            ext = pltpu.roll(ext, -offset, 0)
            return ext[h:]
        return ext[h + offset:h + offset + n]
    h = after.shape[0]
    ext = jnp.concatenate([x, after], axis=0)
    if offset % SUBLANES:
        ext = pltpu.roll(ext, n + h - offset, 0)
        return ext[:n]
    return ext[offset:offset + n]


def _halo_rows(ref, keep):
    return jnp.where(keep, ref[0], 0.0)


def _chunk_index(j, n_chunks, rev):
    return (n_chunks - 1 - j) if rev else j


def _cparams(sem):
    return pltpu.CompilerParams(dimension_semantics=sem)


def _mods_kernel(c_ref, w_ref, b_ref, o_ref):
    o_ref[0] = _dot_f32(_silu(c_ref[...]), w_ref[0]) + b_ref[0]


def _mods(cvec, mod_w, mod_b):
    depth = mod_w.shape[0]
    rows = cvec.shape[0]
    n_tile = 1536
    return pl.pallas_call(
        _mods_kernel,
        grid=(depth, N_MOD * D_MODEL // n_tile),
        in_specs=[
            pl.BlockSpec((rows, D_MODEL), lambda l, n: (0, 0)),
            pl.BlockSpec((1, D_MODEL, n_tile), lambda l, n: (l, 0, n)),
            pl.BlockSpec((1, 1, n_tile), lambda l, n: (l, 0, n)),
        ],
        out_specs=pl.BlockSpec((1, rows, n_tile), lambda l, n: (l, 0, n)),
        out_shape=jax.ShapeDtypeStruct((depth, rows, N_MOD * D_MODEL), F32),
        compiler_params=_cparams(("arbitrary", "arbitrary")),
        name="mods",
    )(cvec, mod_w, mod_b.reshape(depth, 1, N_MOD * D_MODEL))


def _inproj_kernel(x_ref, g_ref, sc_ref, sh_ref, w_ref, oa_ref, ob_ref, oc_ref):
    x = x_ref[0]
    xn = x * lax.rsqrt(jnp.mean(x * x, axis=-1, keepdims=True) + EPS) * g_ref[...]
    xm = xn * (1.0 + sc_ref[0]) + sh_ref[0]
    p = _dot(xm, w_ref[...])
    oa_ref[0] = p[:, :S5_WIDTH]
    ob_ref[0] = p[:, S5_WIDTH:S5_WIDTH + GDN_COLS_PACKED]
    oc_ref[0] = p[:, S5_WIDTH + GDN_COLS_PACKED:]


def _inproj(x, gain, sc, sh, w_packed):
    b, l, _ = x.shape
    tm = min(ROW_TILE, l)
    cols = w_packed.shape[1]
    vec = pl.BlockSpec((1, 1, D_MODEL), lambda i, t: (i, 0, 0))
    return pl.pallas_call(
        _inproj_kernel,
        grid=(b, l // tm),
        in_specs=[
            pl.BlockSpec((1, tm, D_MODEL), lambda i, t: (i, t, 0)),
            pl.BlockSpec((1, D_MODEL), lambda i, t: (0, 0)),
            vec, vec,
            pl.BlockSpec((D_MODEL, cols), lambda i, t: (0, 0), pipeline_mode=pl.Buffered(1)),
        ],
        out_specs=[
            pl.BlockSpec((1, tm, S5_WIDTH), lambda i, t: (i, t, 0)),
            pl.BlockSpec((1, tm, GDN_COLS_PACKED), lambda i, t: (i, t, 0)),
            pl.BlockSpec((1, tm, RWKV_COLS), lambda i, t: (i, t, 0)),
        ],
        out_shape=[
            jax.ShapeDtypeStruct((b, l, S5_WIDTH), F32),
            jax.ShapeDtypeStruct((b, l, GDN_COLS_PACKED), F32),
            jax.ShapeDtypeStruct((b, l, RWKV_COLS), F32),
        ],
        compiler_params=_cparams(("parallel", "arbitrary")),
        name="inproj",
    )(x, gain.reshape(1, D_MODEL), sc, sh, w_packed)


def _outproj_kernel(x_ref, ya_ref, yb_ref, yc_ref, woa_ref, wob_ref, woc_ref, gt1_ref,
                    g_ref, sc_ref, sh_ref, gt2_ref, w1_ref, w2_ref, gf_ref, o_ref, *, final):
    mix = (_dot(ya_ref[0], woa_ref[...]) + _dot(yb_ref[0], wob_ref[...])
           + _dot(yc_ref[0], woc_ref[...]))
    x1 = x_ref[0] + gt1_ref[0] * mix
    h = x1 * lax.rsqrt(jnp.mean(x1 * x1, axis=-1, keepdims=True) + EPS) * g_ref[...]
    h = h * (1.0 + sc_ref[0]) + sh_ref[0]
    a = jnp.maximum(_dot(h, w1_ref[...]), 0.0)
    x2 = x1 + gt2_ref[0] * _dot(a * a, w2_ref[...])
    if final:
        x2 = x2 * lax.rsqrt(jnp.mean(x2 * x2, axis=-1, keepdims=True) + EPS) * gf_ref[...]
    o_ref[0] = x2


def _outproj(x, ya, yb, yc, wo_parts, gt1, gain, sc, sh, gt2, w1, w2, gain_final, final):
    b, l, _ = x.shape
    tm = min(ROW_TILE, l)
    vec = pl.BlockSpec((1, 1, D_MODEL), lambda i, t: (i, 0, 0))
    row = pl.BlockSpec((1, D_MODEL), lambda i, t: (0, 0))

    def tok(width):
        return pl.BlockSpec((1, tm, width), lambda i, t: (i, t, 0))

    def full(arr):
        return pl.BlockSpec(arr.shape, lambda i, t: (0, 0), pipeline_mode=pl.Buffered(1))

    woa, wob, woc = wo_parts
    return pl.pallas_call(
        functools.partial(_outproj_kernel, final=final),
        grid=(b, l // tm),
        in_specs=[tok(D_MODEL), tok(S5_WIDTH), tok(WIDTH), tok(WIDTH),
                  full(woa), full(wob), full(woc), vec, row, vec, vec, vec,
                  full(w1), full(w2), row],
        out_specs=tok(D_MODEL),
        out_shape=jax.ShapeDtypeStruct((b, l, D_MODEL), F32),
        compiler_params=_cparams(("parallel", "arbitrary")),
        name="outproj_mlp",
    )(x, ya, yb, yc, woa, wob, woc, gt1, gain.reshape(1, D_MODEL), sc, sh, gt2, w1, w2,
      gain_final.reshape(1, D_MODEL))


def _cmul(a, b):
    return a[0] * b[0] - a[1] * b[1], a[0] * b[1] + a[1] * b[0]


def _s5_kernel(*refs, rev, readout, n_chunks):
    (u_ref, are_ref, aim_ref, ldt_ref, bre_ref, bim_ref, cre_ref, cim_ref,
     h0re_ref, h0im_ref) = refs[:10]
    if readout:
        d_ref, gw_ref, gb_ref, yf_ref = refs[10:14]
        y_ref, hre_ref, him_ref = refs[14:17]
        wre, wim, lvl, pw, carry = refs[17:]
    else:
        y_ref, hre_ref, him_ref = refs[10:13]
        wre, wim, lvl, pw, carry = refs[13:]
    t = u_ref.shape[1]
    j = pl.program_id(1)
    tile = (SUBLANES, S5_N)

    @pl.when(j == 0)
    def _discretise():
        dt = jnp.exp(ldt_ref[...])
        a_re = are_ref[...]
        a_im = aim_ref[...]
        mag = jnp.exp(a_re * dt)
        ab = (mag * jnp.cos(a_im * dt), mag * jnp.sin(a_im * dt))
        den = a_re * a_re + a_im * a_im
        xr = ab[0] - 1.0
        cr = (xr * a_re + ab[1] * a_im) / den
        ci = (ab[1] * a_re - xr * a_im) / den
        wre[...] = (bre_ref[...] * cr - bim_ref[...] * ci).astype(BF16)
        wim[...] = (bre_ref[...] * ci + bim_ref[...] * cr).astype(BF16)
        powers = [ab]
        for _ in range(SUBLANES - 1):
            powers.append(_cmul(powers[-1], ab))
        row = lax.broadcasted_iota(jnp.int32, tile, 0)
        for li, s in enumerate((1, 2, 4)):
            keep = (row < SUBLANES - s) if rev else (row >= s)
            lvl[2 * li] = jnp.where(keep, powers[s - 1][0], 0.0)
            lvl[2 * li + 1] = jnp.where(keep, powers[s - 1][1], 0.0)
        pr = jnp.zeros(tile, F32)
        pi = jnp.zeros(tile, F32)
        for r in range(SUBLANES):
            k = (SUBLANES - 1 - r) if rev else r
            pr = jnp.where(row == r, powers[k][0], pr)
            pi = jnp.where(row == r, powers[k][1], pi)
        pw[0] = pr
        pw[1] = pi
        carry[0:1, :] = h0re_ref[0]
        carry[1:2, :] = h0im_ref[0]

    u = u_ref[0]
    bur = _dot(u, wre[...])
    bui = _dot(u, wim[...])
    c_re = carry[0:1, :]
    c_im = carry[1:2, :]
    n_tiles = t // SUBLANES
    h_re = [None] * n_tiles
    h_im = [None] * n_tiles
    for i in (range(n_tiles - 1, -1, -1) if rev else range(n_tiles)):
        tr = bur[i * SUBLANES:(i + 1) * SUBLANES]
        ti = bui[i * SUBLANES:(i + 1) * SUBLANES]
        for li, s in enumerate((1, 2, 4)):
            sh = (SUBLANES - s) if rev else s
            sr = pltpu.roll(tr, sh, 0)
            si = pltpu.roll(ti, sh, 0)
            lr = lvl[2 * li]
            lim = lvl[2 * li + 1]
            tr, ti = tr + lr * sr - lim * si, ti + lr * si + lim * sr
        cbr = jnp.broadcast_to(c_re, tile)
        cbi = jnp.broadcast_to(c_im, tile)
        hr = tr + pw[0] * cbr - pw[1] * cbi
        hi = ti + pw[0] * cbi + pw[1] * cbr
        edge = 0 if rev else SUBLANES - 1
        c_re = hr[edge:edge + 1]
        c_im = hi[edge:edge + 1]
        h_re[i] = hr
        h_im[i] = hi
    carry[0:1, :] = c_re
    carry[1:2, :] = c_im
    y = (_dot(jnp.concatenate(h_re, axis=0), cre_ref[...])
         - _dot(jnp.concatenate(h_im, axis=0), cim_ref[...]))
    if readout:
        y = y + yf_ref[0] + d_ref[...] * u
        z = 0.5 * y * (1.0 + jnp.tanh(math.sqrt(2.0 / math.pi) * (y + 0.044715 * y * y * y)))
        y = z * _sigmoid(_dot(z, gw_ref[...]) + gb_ref[...])
    y_ref[0] = y

    @pl.when(j == n_chunks - 1)
    def _emit_state():
        hre_ref[0] = c_re
        him_ref[0] = c_im


def _s5_call(u, prm, d, h0, rev, readout, yf=None):
    b, l, _ = u.shape
    t = min(S5_BLOCK, l)
    n_chunks = l // t
    tok = pl.BlockSpec((1, t, S5_WIDTH), lambda i, j: (i, _chunk_index(j, n_chunks, rev), 0))
    row_n = pl.BlockSpec((1, S5_N), lambda i, j: (0, 0))
    st = pl.BlockSpec((1, 1, S5_N), lambda i, j: (i, 0, 0))

    def full(arr):
        return pl.BlockSpec(arr.shape, lambda i, j: (0,) * arr.ndim)

    args = [u, prm["a_re"][d], prm["a_im"][d], prm["log_dt"][d], prm["b_re"], prm["b_im"],
            prm["c_re"], prm["c_im"], h0[0], h0[1]]
    specs = [tok, row_n, row_n, row_n, full(prm["b_re"]), full(prm["b_im"]),
             full(prm["c_re"]), full(prm["c_im"]), st, st]
    if readout:
        args += [prm["d"], prm["glu_w"], prm["glu_b"], yf]
        specs += [full(prm["d"]), full(prm["glu_w"]), full(prm["glu_b"]), tok]
    y, hre, him = pl.pallas_call(
        functools.partial(_s5_kernel, rev=rev, readout=readout, n_chunks=n_chunks),
        grid=(b, n_chunks),
        in_specs=specs,
        out_specs=[tok, st, st],
        out_shape=[jax.ShapeDtypeStruct((b, l, S5_WIDTH), F32),
                   jax.ShapeDtypeStruct((b, 1, S5_N), F32),
                   jax.ShapeDtypeStruct((b, 1, S5_N), F32)],
        scratch_shapes=[pltpu.VMEM((S5_WIDTH, S5_N), BF16), pltpu.VMEM((S5_WIDTH, S5_N), BF16),
                        pltpu.VMEM((6, SUBLANES, S5_N), F32), pltpu.VMEM((2, SUBLANES, S5_N), F32),
                        pltpu.VMEM((2, S5_N), F32)],
        compiler_params=_cparams(("parallel", "arbitrary")),
        name="s5_bwd" if rev else "s5_fwd",
    )(*args)
    return y, (hre, him)


def _s5_mixer(u_ctx, u_lat, prm):
    b = u_lat.shape[0]
    zero = (jnp.zeros((b, 1, S5_N), F32), jnp.zeros((b, 1, S5_N), F32))
    yc_f, hc_f = _s5_call(u_ctx, prm, 0, zero, False, False)
    y_ctx, hc_b = _s5_call(u_ctx, prm, 1, zero, True, True, yc_f)
    yl_f, _ = _s5_call(u_lat, prm, 0, hc_f, False, False)
    y_lat, _ = _s5_call(u_lat, prm, 1, hc_b, True, True, yl_f)
    return y_ctx, y_lat


def _block_order(nc, rev):
    return range(nc - 1, -1, -1) if rev else range(nc)


def _store_heads(buf, ci, o):
    for h in range(HEADS):
        buf[ci * CHUNK:(ci + 1) * CHUNK, h * HEAD:(h + 1) * HEAD] = o[h]


def _gdn_kernel(*refs, rev, readout, n_blocks, d):
    pp_ref, pc_ref, pn_ref, cw_ref, alog_ref, dtb_ref, ones_ref, s0_ref = refs[:8]
    if readout:
        nw_ref, of_ref = refs[8:10]
        y_ref, so_ref = refs[10:12]
        state, obuf = refs[12:]
    else:
        y_ref, so_ref = refs[8:10]
        state, obuf = refs[10:]
    bt = pc_ref.shape[1]
    nc = bt // CHUNK
    j = pl.program_id(1)
    c = _chunk_index(j, n_blocks, rev)

    @pl.when(j == 0)
    def _load_state():
        state[...] = s0_ref[0]

    x = pc_ref[0]
    qkv = x[:, :3 * WIDTH]
    before = _halo_rows(pp_ref, c > 0)[:, :3 * WIDTH]
    after = _halo_rows(pn_ref, c < n_blocks - 1)[:, :3 * WIDTH]
    half = GDN_CONV // 2
    acc = cw_ref[half:half + 1, :] * qkv
    for off in range(-half, half + 1):
        if off != 0:
            acc = acc + cw_ref[half + off:half + off + 1, :] * _shift_rows(qkv, before, after, off)
    act = _silu(acc)
    q = act[:, :WIDTH]
    k = act[:, WIDTH:2 * WIDTH]
    v = act[:, 2 * WIDTH:]
    q = q * (lax.rsqrt(_head_sum(q * q, ones_ref) + EPS) * HEAD ** -0.5)
    k = k * lax.rsqrt(_head_sum(k * k, ones_ref) + EPS)
    tail = x[:, 4 * WIDTH:]
    beta_all = _sigmoid(tail)
    g_all = -jnp.exp(alog_ref[...]) * _softplus(tail + dtb_ref[...])
    gcs = _dot_f32(_chunk_cumsum_matrix(bt, rev), g_all)
    gcs_t = gcs.T
    first_g = 2 * HEADS + d * HEADS
    gc = _split_cols(gcs, first_g)
    gr = jnp.stack([gcs_t[first_g + h:first_g + h + 1, ci * CHUNK:(ci + 1) * CHUNK]
                    for ci in range(nc) for h in range(HEADS)], axis=0)
    beta = _split_cols(beta_all, d * HEADS)
    qs = _split_heads(q)
    ks = _split_heads(k)
    vs = _split_heads(v)
    incl = _tri_mask(rev, False)
    strict = _tri_mask(rev, True)
    decay = jnp.where(incl, jnp.exp(jnp.minimum(gc - gr, 0.0)), 0.0)
    kb = ks * beta
    big = _bmm(jnp.concatenate([kb, qs], axis=1), ks, tb=True)
    m = jnp.where(strict, big[:, :CHUNK] * decay, 0.0)
    attn = big[:, CHUNK:] * decay
    tinv = _inv_unit_tri(m)
    eg = jnp.exp(gc)
    uw = _bmm(tinv, jnp.concatenate([vs * beta, kb * eg], axis=2))
    edge = 0 if rev else CHUNK - 1
    g_last = gc[:, edge:edge + 1]
    kd = ks * jnp.exp(g_last - gc)
    au = _bmm(attn, uw)
    q_eff = qs * eg - au[:, :, HEAD:]
    y_in = au[:, :, :HEAD]
    kdu = _bmm(kd, uw, ta=True)
    dec_last = jnp.exp(g_last)
    s = state[...]
    for ci in _block_order(nc, rev):
        sl = slice(ci * HEADS, (ci + 1) * HEADS)
        _store_heads(obuf, ci, _bmm(q_eff[sl], s) + y_in[sl])
        s = s * dec_last[sl] - _bmm(kdu[sl, :, HEAD:], s) + kdu[sl, :, :HEAD]
    state[...] = s
    o = obuf[...]
    if readout:
        o = o + of_ref[0]
        z = x[:, 3 * WIDTH:4 * WIDTH]
        scale = lax.rsqrt(_head_sum(o * o, ones_ref) * (1.0 / HEAD) + EPS)
        o = o * scale * nw_ref[...] * _silu(z)
    y_ref[0] = o

    @pl.when(j == n_blocks - 1)
    def _emit_state():
        so_ref[0] = state[...]


def _block_rows(l):
    return min(BLOCK_CHUNKS * CHUNK, l)


def _halo_specs(width, bt, n_blocks, rev, halo):
    per_block = bt // halo
    last = n_blocks * per_block - 1

    def cur(i, j):
        return (i, _chunk_index(j, n_blocks, rev), 0)

    def prev(i, j):
        return (i, jnp.maximum(_chunk_index(j, n_blocks, rev) * per_block - 1, 0), 0)

    def nxt(i, j):
        return (i, jnp.minimum((_chunk_index(j, n_blocks, rev) + 1) * per_block, last), 0)

    return [pl.BlockSpec((1, halo, width), prev), pl.BlockSpec((1, bt, width), cur),
            pl.BlockSpec((1, halo, width), nxt)]


def _gdn_call(p, prm, d, s0, rev, readout, of=None):
    b, l, _ = p.shape
    bt = _block_rows(l)
    n_blocks = l // bt
    tok = pl.BlockSpec((1, bt, WIDTH), lambda i, j: (i, _chunk_index(j, n_blocks, rev), 0))
    st = pl.BlockSpec((1, HEADS, HEAD, HEAD), lambda i, j: (i, 0, 0, 0))

    def full(arr):
        return pl.BlockSpec(arr.shape, lambda i, j: (0,) * arr.ndim)

    args = [p, p, p, prm["conv"], prm["alog"][d], prm["dtb"][d], prm["ones"], s0]
    specs = _halo_specs(GDN_COLS_PACKED, bt, n_blocks, rev, SUBLANES) + [
        full(prm["conv"]), full(prm["alog"][d]), full(prm["dtb"][d]), full(prm["ones"]), st]
    if readout:
        args += [prm["norm"], of]
        specs += [full(prm["norm"]), tok]
    return pl.pallas_call(
        functools.partial(_gdn_kernel, rev=rev, readout=readout, n_blocks=n_blocks, d=d),
        grid=(b, n_blocks),
        in_specs=specs,
        out_specs=[tok, st],
        out_shape=[jax.ShapeDtypeStruct((b, l, WIDTH), F32),
                   jax.ShapeDtypeStruct((b, HEADS, HEAD, HEAD), F32)],
        scratch_shapes=[pltpu.VMEM((HEADS, HEAD, HEAD), F32), pltpu.VMEM((bt, WIDTH), F32)],
        compiler_params=_cparams(("parallel", "arbitrary")),
        name="gdn_bwd" if rev else "gdn_fwd",
    )(*args)


def _gdn_mixer(p_ctx, p_lat, prm):
    b = p_lat.shape[0]
    zero = jnp.zeros((b, HEADS, HEAD, HEAD), F32)
    oc_f, sc_f = _gdn_call(p_ctx, prm, 0, zero, False, False)
    y_ctx, sc_b = _gdn_call(p_ctx, prm, 1, zero, True, True, oc_f)
    ol_f, _ = _gdn_call(p_lat, prm, 0, sc_f, False, False)
    y_lat, _ = _gdn_call(p_lat, prm, 1, sc_b, True, True, ol_f)
    return y_ctx, y_lat


def _rwkv_kernel(*refs, rev, readout, grid_shift, n_blocks):
    (pp_ref, pc_ref, pn_ref, mu_ref, w0_ref, wup_ref, a0_ref, aup_ref, kk_ref, ka_ref, rk_ref,
     ones_ref, s0_ref) = refs[:13]
    if readout:
        gup_ref, lnw_ref, lnb_ref, yf_ref, bf_ref = refs[13:18]
        y_ref, so_ref = refs[18:20]
        state, obuf = refs[20:]
    else:
        y_ref, bon_ref, so_ref = refs[13:16]
        state, obuf = refs[16:]
    bt = pc_ref.shape[1]
    nc = bt // CHUNK
    j = pl.program_id(1)
    c = _chunk_index(j, n_blocks, rev)

    @pl.when(j == 0)
    def _load_state():
        state[...] = s0_ref[0]

    x = pc_ref[0]
    before = _halo_rows(pp_ref, c > 0)
    after = _halo_rows(pn_ref, c < n_blocks - 1)
    lane = lax.broadcasted_iota(jnp.int32, x.shape, 1)
    if grid_shift:
        col = lax.broadcasted_iota(jnp.int32, x.shape, 0) & (GRID_W - 1)
        left = jnp.where(col == 0, 0.0, pltpu.roll(x, 1, 0))
        right = jnp.where(col == GRID_W - 1, 0.0, pltpu.roll(x, bt - 1, 0))
        up = _shift_rows(x, before, after, -GRID_W)
        down = _shift_rows(x, before, after, GRID_W)
        cm = lane & 3
        shifted = jnp.where(cm == 0, left, jnp.where(cm == 1, right, jnp.where(cm == 2, up, down)))
    else:
        shifted = jnp.where((lane & 1) == 0, _shift_rows(x, before, after, -1),
                            _shift_rows(x, before, after, 1))
    pm = x + (shifted - x) * mu_ref[...]
    r = pm[:, :WIDTH]
    k = pm[:, WIDTH:2 * WIDTH]
    v = pm[:, 2 * WIDTH:3 * WIDTH]
    o1 = 3 * WIDTH
    w_dn = pm[:, o1:o1 + 2 * LORA]
    a_dn = pm[:, o1 + 2 * LORA:o1 + 4 * LORA]
    g_dn = pm[:, o1 + 4 * LORA:]
    w_log = -_softplus(-(w0_ref[...] + _dot(jnp.tanh(w_dn), wup_ref[...]))) - 0.5
    lw = -jnp.exp(w_log)
    iclr = _sigmoid(a0_ref[...] + _dot(a_dn, aup_ref[...]))
    kkr = k * kk_ref[...]
    kk = kkr * lax.rsqrt(_head_sum(kkr * kkr, ones_ref) + EPS)
    k_dir = k * (1.0 + (iclr - 1.0) * ka_ref[...])
    b_dir = kk * iclr
    bonus = _head_sum(r * k_dir * rk_ref[...], ones_ref) * v

    g_inc = _dot_f32(_chunk_cumsum_matrix(bt, rev), lw)
    g_end = _chunk_edge_rows(g_inc, rev)
    e_neg = jnp.exp(-g_inc)
    e_end = jnp.exp(g_end - g_inc)
    kap_h = _split_heads(kk * jnp.exp(g_inc - lw))
    r_h = _split_heads(r * jnp.exp(g_inc))
    b_h = _split_heads(b_dir * e_neg)
    k_h = _split_heads(k_dir * e_neg)
    b_eh = _split_heads(b_dir * e_end)
    k_eh = _split_heads(k_dir * e_end)
    v_h = _split_heads(v)
    dec_end = _split_heads(jnp.exp(g_end))[:, :1, :]
    incl = _tri_mask(rev, False)
    strict = _tri_mask(rev, True)
    big = _bmm(jnp.concatenate([kap_h, r_h], axis=1), jnp.concatenate([b_h, k_h], axis=1),
               tb=True)
    a_b = jnp.where(strict, big[:, :CHUNK, :CHUNK], 0.0)
    a_k = jnp.where(strict, big[:, :CHUNK, CHUNK:], 0.0)
    r_b = jnp.where(incl, big[:, CHUNK:, :CHUNK], 0.0)
    r_k = jnp.where(incl, big[:, CHUNK:, CHUNK:], 0.0)
    tinv = _inv_unit_tri(a_b)
    z = _bmm(tinv, jnp.concatenate([kap_h, _bmm(a_k, v_h)], axis=2))
    rz = _bmm(r_b, z)
    q_eff = r_h - rz[:, :, :HEAD]
    y_in = _bmm(r_k, v_h) - rz[:, :, HEAD:]
    ztb = _bmm(z, b_eh, ta=True)
    pt = ztb[:, :HEAD]
    xt = _bmm(v_h, k_eh, ta=True) - ztb[:, HEAD:]
    s = state[...]
    for ci in _block_order(nc, rev):
        sl = slice(ci * HEADS, (ci + 1) * HEADS)
        _store_heads(obuf, ci, _bmm(q_eff[sl], s, tb=True) + y_in[sl])
        s = s * dec_end[sl] - _bmm(s, pt[sl]) + xt[sl]
    state[...] = s
    y = obuf[...]
    if readout:
        y = y + yf_ref[0]
        yc = y - _head_sum(y, ones_ref) * (1.0 / HEAD)
        var = _head_sum(yc * yc, ones_ref) * (1.0 / HEAD)
        yn = yc * lax.rsqrt(var + RWKV_GN_EPS) * lnw_ref[...] + lnb_ref[...]
        gate = _dot(_sigmoid(g_dn), gup_ref[...])
        y_ref[0] = (yn + bonus + bf_ref[0]) * gate
    else:
        y_ref[0] = y
        bon_ref[0] = bonus

    @pl.when(j == n_blocks - 1)
    def _emit_state():
        so_ref[0] = state[...]


def _rwkv_call(p, prm, d, s0, rev, readout, grid_shift, yf=None, bf=None):
    b, l, _ = p.shape
    bt = _block_rows(l)
    n_blocks = l // bt
    tok = pl.BlockSpec((1, bt, WIDTH), lambda i, j: (i, _chunk_index(j, n_blocks, rev), 0))
    st = pl.BlockSpec((1, HEADS, HEAD, HEAD), lambda i, j: (i, 0, 0, 0))

    def full(arr):
        return pl.BlockSpec(arr.shape, lambda i, j: (0,) * arr.ndim)

    small = [prm["mu"], prm["w0"][d], prm["w_up"][d], prm["a0"][d], prm["a_up"][d],
             prm["k_k"], prm["k_a"], prm["r_k"], prm["ones"]]
    args = [p, p, p] + small + [s0]
    halo = GRID_W if grid_shift else SUBLANES
    specs = _halo_specs(RWKV_COLS, bt, n_blocks, rev, halo) + [full(a) for a in small] + [st]
    tok_shape = jax.ShapeDtypeStruct((b, l, WIDTH), F32)
    st_shape = jax.ShapeDtypeStruct((b, HEADS, HEAD, HEAD), F32)
    if readout:
        extra = [prm["g_up"], prm["ln_w"], prm["ln_b"]]
        args += extra + [yf, bf]
        specs += [full(a) for a in extra] + [tok, tok]
        out_specs, out_shape = [tok, st], [tok_shape, st_shape]
    else:
        out_specs, out_shape = [tok, tok, st], [tok_shape, tok_shape, st_shape]
    return pl.pallas_call(
        functools.partial(_rwkv_kernel, rev=rev, readout=readout, grid_shift=grid_shift,
                          n_blocks=n_blocks),
        grid=(b, n_blocks),
        in_specs=specs,
        out_specs=out_specs,
        out_shape=out_shape,
        scratch_shapes=[pltpu.VMEM((HEADS, HEAD, HEAD), F32), pltpu.VMEM((bt, WIDTH), F32)],
        compiler_params=_cparams(("parallel", "arbitrary")),
        name="rwkv_bwd" if rev else "rwkv_fwd",
    )(*args)


def _rwkv_mixer(p_ctx, p_lat, prm):
    b = p_lat.shape[0]
    zero = jnp.zeros((b, HEADS, HEAD, HEAD), F32)
    yc_f, bc_f, sc_f = _rwkv_call(p_ctx, prm, 0, zero, False, False, False)
    y_ctx, sc_b = _rwkv_call(p_ctx, prm, 1, zero, True, True, False, yc_f, bc_f)
    yl_f, bl_f, _ = _rwkv_call(p_lat, prm, 0, sc_f, False, False, True)
    y_lat, _ = _rwkv_call(p_lat, prm, 1, sc_b, True, True, True, yl_f, bl_f)
    return y_ctx, y_lat


def _pack_w_in(w):
    o_gdn = S5_WIDTH
    o_tail = o_gdn + 4 * WIDTH
    o_rwkv = o_tail + 4 * HEADS
    pad = jnp.zeros((w.shape[0], GDN_TAIL - 4 * HEADS), w.dtype)
    return jnp.concatenate([w[:, :o_tail], w[:, o_tail:o_rwkv], pad, w[:, o_rwkv:]], axis=1)


def _block_diag(blocks):
    g, r, c = blocks.shape
    eye = jnp.eye(g, dtype=blocks.dtype)
    return (eye[:, None, :, None] * blocks[:, :, None, :]).reshape(g * r, g * c)


def _s5_params(layer, b_re, b_im, c_re, c_im, d, a_re, a_im, log_dt, glu_w, glu_b):
    row = lambda v: v.reshape(2, 1, S5_N)
    return {
        "a_re": row(a_re[layer]), "a_im": row(a_im[layer]),
        "log_dt": row(jnp.repeat(log_dt[layer], S5_STATE, axis=-1)),
        "b_re": _block_diag(jnp.swapaxes(b_re[layer], 1, 2)),
        "b_im": _block_diag(jnp.swapaxes(b_im[layer], 1, 2)),
        "c_re": _block_diag(jnp.swapaxes(c_re[layer], 1, 2)),
        "c_im": _block_diag(jnp.swapaxes(c_im[layer], 1, 2)),
        "d": d[layer].reshape(1, S5_WIDTH), "glu_w": glu_w[layer],
        "glu_b": glu_b[layer].reshape(1, S5_WIDTH),
    }


def _head_ones():
    return _block_diag(jnp.ones((HEADS, HEAD, HEAD), BF16))


def _gdn_params(layer, conv, a_log, dt_bias, norm):
    def tail_row(v):
        rows = []
        for d in range(2):
            z = jnp.zeros((GDN_TAIL,), F32)
            rows.append(z.at[2 * HEADS + d * HEADS:2 * HEADS + (d + 1) * HEADS].set(v[d])
                        .reshape(1, GDN_TAIL))
        return rows

    return {"conv": conv[layer], "alog": tail_row(a_log[layer]), "dtb": tail_row(dt_bias[layer]),
            "norm": jnp.tile(norm[layer], HEADS).reshape(1, WIDTH), "ones": _head_ones()}


def _rwkv_params(layer, mu, w0, w_up, a0, a_up, g_up, k_k, k_a, r_k, ln_w, ln_b):
    def lora_pad(w):
        z = jnp.zeros((LORA, WIDTH), F32)
        return [jnp.concatenate([w[0], z], axis=0), jnp.concatenate([z, w[1]], axis=0)]

    row = lambda v: v.reshape(1, -1)
    return {
        "mu": row(mu[layer]), "w0": [row(w0[layer, 0]), row(w0[layer, 1])],
        "w_up": lora_pad(w_up[layer]), "a0": [row(a0[layer, 0]), row(a0[layer, 1])],
        "a_up": lora_pad(a_up[layer]), "g_up": g_up[layer], "k_k": row(k_k[layer]),
        "k_a": row(k_a[layer]), "r_k": row(r_k[layer]), "ln_w": row(ln_w[layer]),
        "ln_b": row(ln_b[layer]), "ones": _head_ones(),
    }


def kernel(x, c, ctx, c_ctx, mod_w, mod_b, norm_mix, norm_mlp, norm_final, w_in, w_out, s5_b_re, s5_b_im, s5_c_re, s5_c_im, s5_d, s5_a_re, s5_a_im, s5_log_dt, s5_glu_w, s5_glu_b, gdn_conv, gdn_a_log, gdn_dt_bias, gdn_norm, rwkv_mu, rwkv_w0, rwkv_w_up, rwkv_a0, rwkv_a_up, rwkv_g_up, rwkv_k_k, rwkv_k_a, rwkv_r_k, rwkv_ln_w, rwkv_ln_b, mlp_w1, mlp_w2):
    depth = mod_w.shape[0]
    batch = x.shape[0]
    x = x.astype(F32)
    ctx = ctx.astype(F32)
    rows = SUBLANES * ((batch + 1 + SUBLANES - 1) // SUBLANES)
    cvec = jnp.zeros((rows, D_MODEL), F32).at[:batch].set(c.astype(F32)).at[batch].set(
        c_ctx.astype(F32))
    mods_all = _mods(cvec, mod_w, mod_b)

    for layer in range(depth):
        ctx_out = layer < depth - 1
        m = mods_all[layer].reshape(rows, N_MOD, 1, D_MODEL)
        lat = [m[:batch, i] for i in range(N_MOD)]
        cm = [jnp.broadcast_to(m[batch, i][None], (batch, 1, D_MODEL)) for i in range(N_MOD)]
        w_packed = _pack_w_in(w_in[layer]).astype(BF16)
        wo = w_out[layer].astype(BF16)
        wo_parts = (wo[:S5_WIDTH], wo[S5_WIDTH:S5_WIDTH + WIDTH], wo[S5_WIDTH + WIDTH:])
        w1 = mlp_w1[layer].astype(BF16)
        w2 = mlp_w2[layer].astype(BF16)

        pa_l, pb_l, pc_l = _inproj(x, norm_mix[layer], lat[1], lat[0], w_packed)
        pa_c, pb_c, pc_c = _inproj(ctx, norm_mix[layer], cm[1], cm[0], w_packed)

        ya_c, ya_l = _s5_mixer(pa_c, pa_l, _s5_params(
            layer, s5_b_re, s5_b_im, s5_c_re, s5_c_im, s5_d, s5_a_re, s5_a_im, s5_log_dt,
            s5_glu_w, s5_glu_b))
        yb_c, yb_l = _gdn_mixer(pb_c, pb_l, _gdn_params(
            layer, gdn_conv, gdn_a_log, gdn_dt_bias, gdn_norm))
        yc_c, yc_l = _rwkv_mixer(pc_c, pc_l, _rwkv_params(
            layer, rwkv_mu, rwkv_w0, rwkv_w_up, rwkv_a0, rwkv_a_up, rwkv_g_up, rwkv_k_k,
            rwkv_k_a, rwkv_r_k, rwkv_ln_w, rwkv_ln_b))

        x = _outproj(x, ya_l, yb_l, yc_l, wo_parts, lat[2], norm_mlp[layer], lat[4], lat[3],
                     lat[5], w1, w2, norm_final, final=not ctx_out)
        if ctx_out:
            ctx = _outproj(ctx, ya_c, yb_c, yc_c, wo_parts, cm[2], norm_mlp[layer], cm[4], cm[3],
                           cm[5], w1, w2, norm_final, final=False)
    return x
```

```python
import functools
import math

import jax
import jax.numpy as jnp
from jax import lax
from jax.experimental import pallas as pl
from jax.experimental.pallas import tpu as pltpu

F32 = jnp.float32
BF16 = jnp.bfloat16
HIGHEST = lax.Precision.HIGHEST

D_MODEL = 1024
D_FF = 4 * D_MODEL
N_MOD = 6
EPS = 1e-6
GRID_W = 64

S5_WIDTH = 256
S5_GROUP = 16
S5_GROUPS = 16
S5_STATE = 64
S5_N = S5_GROUPS * S5_STATE
S5_BLOCK = 256

HEAD = 64
HEADS = 6
WIDTH = HEADS * HEAD
GDN_CONV = 5
GDN_TAIL = 128
GDN_COLS_PACKED = 4 * WIDTH + GDN_TAIL
LORA = 64
GATE_LORA = 128
RWKV_COLS = 3 * WIDTH + 2 * LORA + 2 * LORA + GATE_LORA
RWKV_GN_EPS = 64e-5
CHUNK = 64
SUBLANES = 8
BLOCK_CHUNKS = 16
SUB_CHUNKS = 4
CHAIN_LEAD = 2

ROW_TILE = 512


def _dot(a, b, dims=((1,), (0,))):
    return lax.dot_general(a.astype(BF16), b.astype(BF16), (dims, ((), ())),
                           preferred_element_type=F32)


def _dot_f32(a, b, dims=((1,), (0,))):
    return lax.dot_general(a, b, (dims, ((), ())), precision=HIGHEST,
                           preferred_element_type=F32)


_NT = ((1,), (1,))
_TN = ((0,), (0,))


def _sigmoid(x):
    return 0.5 + 0.5 * jnp.tanh(0.5 * x)


def _silu(x):
    h = 0.5 * x
    return h + h * jnp.tanh(h)


def _softplus(x):
    return jnp.maximum(x, 0.0) + jnp.log1p(jnp.exp(-jnp.abs(x)))


def _bmm(a, b, ta=False, tb=False):
    dims = (((1 if ta else 2,), (2 if tb else 1,)), ((0,), (0,)))
    return lax.dot_general(a.astype(BF16), b.astype(BF16), dims, preferred_element_type=F32)


def _tri_mask(rev, strict):
    shape = (1, CHUNK, CHUNK)
    r = lax.broadcasted_iota(jnp.int32, shape, 1)
    c = lax.broadcasted_iota(jnp.int32, shape, 2)
    if rev:
        return (c > r) if strict else (c >= r)
    return (c < r) if strict else (c <= r)


def _chunk_cumsum_matrix(n, rev):
    r = lax.broadcasted_iota(jnp.int32, (n, n), 0)
    c = lax.broadcasted_iota(jnp.int32, (n, n), 1)
    shift = int(math.log2(CHUNK))
    same = (r >> shift) == (c >> shift)
    return jnp.where(same & ((c >= r) if rev else (c <= r)), 1.0, 0.0).astype(F32)


def _inv_unit_tri(a):
    n = a.shape[-1]
    r = lax.broadcasted_iota(jnp.int32, (1, n, n), 1)
    c = lax.broadcasted_iota(jnp.int32, (1, n, n), 2)
    inv = jnp.where(r == c, 1.0, 0.0).astype(F32) - jnp.where((r >> 1) == (c >> 1), a, 0.0)
    for lvl in range(1, int(math.log2(n))):
        couple = ((r >> (lvl + 1)) == (c >> (lvl + 1))) & ((r >> lvl) != (c >> lvl))
        half = _bmm(inv, jnp.where(couple, a, 0.0))
        yield
        inv = inv - _bmm(half, inv)
        yield
    return inv


def _run(gen):
    while True:
        try:
            next(gen)
        except StopIteration as stop:
            return stop.value


def _interleave(main, side, lead):
    results = {}
    live = {"main": main, "side": side}
    turn = 0
    while live:
        for name in ("main", "side"):
            if name in live and not (name == "side" and turn < lead and "main" in live):
                try:
                    next(live[name])
                except StopIteration as stop:
                    results[name] = stop.value
                    del live[name]
        turn += 1
    return results["main"], results["side"]


def _split_heads(x):
    nc = x.shape[0] // CHUNK
    return jnp.stack([x[ci * CHUNK:(ci + 1) * CHUNK, h * HEAD:(h + 1) * HEAD]
                      for ci in range(nc) for h in range(HEADS)], axis=0)


def _split_cols(x, first):
    nc = x.shape[0] // CHUNK
    return jnp.stack([x[ci * CHUNK:(ci + 1) * CHUNK, first + h:first + h + 1]
                      for ci in range(nc) for h in range(HEADS)], axis=0)


def _chunk_edge_rows(x, rev):
    nc = x.shape[0] // CHUNK
    edge = 0 if rev else CHUNK - 1
    return jnp.concatenate(
        [jnp.broadcast_to(x[ci * CHUNK + edge:ci * CHUNK + edge + 1], (CHUNK, x.shape[1]))
         for ci in range(nc)], axis=0)


def _head_sum(x, ones_ref):
    return _dot(x, ones_ref[...])


def _halo_rows(ref, keep):
    return jnp.where(keep, ref[0], 0.0)


def _chunk_index(j, n_chunks, rev):
    return (n_chunks - 1 - j) if rev else j


def _cparams(sem):
    return pltpu.CompilerParams(dimension_semantics=sem)


def _mods_kernel(c_ref, w_ref, b_ref, o_ref):
    o_ref[0] = _dot_f32(_silu(c_ref[...]), w_ref[0]) + b_ref[0]


def _mods(cvec, mod_w, mod_b):
    depth = mod_w.shape[0]
    rows = cvec.shape[0]
    n_tile = 1536
    return pl.pallas_call(
        _mods_kernel,
        grid=(depth, N_MOD * D_MODEL // n_tile),
        in_specs=[
            pl.BlockSpec((rows, D_MODEL), lambda l, n: (0, 0)),
            pl.BlockSpec((1, D_MODEL, n_tile), lambda l, n: (l, 0, n)),
            pl.BlockSpec((1, 1, n_tile), lambda l, n: (l, 0, n)),
        ],
        out_specs=pl.BlockSpec((1, rows, n_tile), lambda l, n: (l, 0, n)),
        out_shape=jax.ShapeDtypeStruct((depth, rows, N_MOD * D_MODEL), F32),
        compiler_params=_cparams(("arbitrary", "arbitrary")),
        name="mods",
    )(cvec, mod_w, mod_b.reshape(depth, 1, N_MOD * D_MODEL))


def _inproj_kernel(x_ref, g_ref, sc_ref, sh_ref, w_ref, oa_ref, ob_ref, oc_ref):
    x = x_ref[0]
    xn = x * lax.rsqrt(jnp.mean(x * x, axis=-1, keepdims=True) + EPS) * g_ref[...]
    xm = xn * (1.0 + sc_ref[0]) + sh_ref[0]
    p = _dot(xm, w_ref[...])
    oa_ref[0] = p[:, :S5_WIDTH]
    ob_ref[0] = p[:, S5_WIDTH:S5_WIDTH + GDN_COLS_PACKED]
    oc_ref[0] = p[:, S5_WIDTH + GDN_COLS_PACKED:]


def _inproj(x, gain, sc, sh, w_packed):
    b, l, _ = x.shape
    tm = min(ROW_TILE, l)
    cols = w_packed.shape[1]
    vec = pl.BlockSpec((1, 1, D_MODEL), lambda i, t: (i, 0, 0))
    return pl.pallas_call(
        _inproj_kernel,
        grid=(b, l // tm),
        in_specs=[
            pl.BlockSpec((1, tm, D_MODEL), lambda i, t: (i, t, 0)),
            pl.BlockSpec((1, D_MODEL), lambda i, t: (0, 0)),
            vec, vec,
            pl.BlockSpec((D_MODEL, cols), lambda i, t: (0, 0), pipeline_mode=pl.Buffered(1)),
        ],
        out_specs=[
            pl.BlockSpec((1, tm, S5_WIDTH), lambda i, t: (i, t, 0)),
            pl.BlockSpec((1, tm, GDN_COLS_PACKED), lambda i, t: (i, t, 0)),
            pl.BlockSpec((1, tm, RWKV_COLS), lambda i, t: (i, t, 0)),
        ],
        out_shape=[
            jax.ShapeDtypeStruct((b, l, S5_WIDTH), F32),
            jax.ShapeDtypeStruct((b, l, GDN_COLS_PACKED), F32),
            jax.ShapeDtypeStruct((b, l, RWKV_COLS), F32),
        ],
        compiler_params=_cparams(("parallel", "arbitrary")),
        name="inproj",
    )(x, gain.reshape(1, D_MODEL), sc, sh, w_packed)


def _outproj_kernel(x_ref, ya_ref, yb_ref, yc_ref, woa_ref, wob_ref, woc_ref, gt1_ref,
                    g_ref, sc_ref, sh_ref, gt2_ref, w1_ref, w2_ref, gf_ref, o_ref, *, final):
    mix = (_dot(ya_ref[0], woa_ref[...]) + _dot(yb_ref[0], wob_ref[...])
           + _dot(yc_ref[0], woc_ref[...]))
    x1 = x_ref[0] + gt1_ref[0] * mix
    h = x1 * lax.rsqrt(jnp.mean(x1 * x1, axis=-1, keepdims=True) + EPS) * g_ref[...]
    h = h * (1.0 + sc_ref[0]) + sh_ref[0]
    a = jnp.maximum(_dot(h, w1_ref[...]), 0.0)
    x2 = x1 + gt2_ref[0] * _dot(a * a, w2_ref[...])
    if final:
        x2 = x2 * lax.rsqrt(jnp.mean(x2 * x2, axis=-1, keepdims=True) + EPS) * gf_ref[...]
    o_ref[0] = x2


def _outproj(x, ya, yb, yc, wo_parts, gt1, gain, sc, sh, gt2, w1, w2, gain_final, final):
    b, l, _ = x.shape
    tm = min(ROW_TILE, l)
    vec = pl.BlockSpec((1, 1, D_MODEL), lambda i, t: (i, 0, 0))
    row = pl.BlockSpec((1, D_MODEL), lambda i, t: (0, 0))

    def tok(width):
        return pl.BlockSpec((1, tm, width), lambda i, t: (i, t, 0))

    def full(arr):
        return pl.BlockSpec(arr.shape, lambda i, t: (0, 0), pipeline_mode=pl.Buffered(1))

    woa, wob, woc = wo_parts
    return pl.pallas_call(
        functools.partial(_outproj_kernel, final=final),
        grid=(b, l // tm),
        in_specs=[tok(D_MODEL), tok(S5_WIDTH), tok(WIDTH), tok(WIDTH),
                  full(woa), full(wob), full(woc), vec, row, vec, vec, vec,
                  full(w1), full(w2), row],
        out_specs=tok(D_MODEL),
        out_shape=jax.ShapeDtypeStruct((b, l, D_MODEL), F32),
        compiler_params=_cparams(("parallel", "arbitrary")),
        name="outproj_mlp",
    )(x, ya, yb, yc, woa, wob, woc, gt1, gain.reshape(1, D_MODEL), sc, sh, gt2, w1, w2,
      gain_final.reshape(1, D_MODEL))


def _cmul(a, b):
    return a[0] * b[0] - a[1] * b[1], a[0] * b[1] + a[1] * b[0]


def _s5_kernel(*refs, rev, readout, n_chunks):
    (u_ref, are_ref, aim_ref, ldt_ref, bre_ref, bim_ref, cre_ref, cim_ref,
     h0re_ref, h0im_ref) = refs[:10]
    if readout:
        d_ref, gw_ref, gb_ref, yf_ref = refs[10:14]
        y_ref, hre_ref, him_ref = refs[14:17]
        wre, wim, lvl, pw, carry = refs[17:]
    else:
        y_ref, hre_ref, him_ref = refs[10:13]
        wre, wim, lvl, pw, carry = refs[13:]
    t = u_ref.shape[1]
    j = pl.program_id(1)
    tile = (SUBLANES, S5_N)

    @pl.when(j == 0)
    def _discretise():
        dt = jnp.exp(ldt_ref[...])
        a_re = are_ref[...]
        a_im = aim_ref[...]
        mag = jnp.exp(a_re * dt)
        ab = (mag * jnp.cos(a_im * dt), mag * jnp.sin(a_im * dt))
        den = a_re * a_re + a_im * a_im
        xr = ab[0] - 1.0
        cr = (xr * a_re + ab[1] * a_im) / den
        ci = (ab[1] * a_re - xr * a_im) / den
        wre[...] = (bre_ref[...] * cr - bim_ref[...] * ci).astype(BF16)
        wim[...] = (bre_ref[...] * ci + bim_ref[...] * cr).astype(BF16)
        powers = [ab]
        for _ in range(SUBLANES - 1):
            powers.append(_cmul(powers[-1], ab))
        row = lax.broadcasted_iota(jnp.int32, tile, 0)
        for li, s in enumerate((1, 2, 4)):
            keep = (row < SUBLANES - s) if rev else (row >= s)
            lvl[2 * li] = jnp.where(keep, powers[s - 1][0], 0.0)
            lvl[2 * li + 1] = jnp.where(keep, powers[s - 1][1], 0.0)
        pr = jnp.zeros(tile, F32)
        pi = jnp.zeros(tile, F32)
        for r in range(SUBLANES):
            k = (SUBLANES - 1 - r) if rev else r
            pr = jnp.where(row == r, powers[k][0], pr)
            pi = jnp.where(row == r, powers[k][1], pi)
        pw[0] = pr
        pw[1] = pi
        carry[0:1, :] = h0re_ref[0]
        carry[1:2, :] = h0im_ref[0]

    u = u_ref[0]
    bur = _dot(u, wre[...])
    bui = _dot(u, wim[...])
    c_re = carry[0:1, :]
    c_im = carry[1:2, :]
    n_tiles = t // SUBLANES
    h_re = [None] * n_tiles
    h_im = [None] * n_tiles
    for i in (range(n_tiles - 1, -1, -1) if rev else range(n_tiles)):
        tr = bur[i * SUBLANES:(i + 1) * SUBLANES]
        ti = bui[i * SUBLANES:(i + 1) * SUBLANES]
        for li, s in enumerate((1, 2, 4)):
            sh = (SUBLANES - s) if rev else s
            sr = pltpu.roll(tr, sh, 0)
            si = pltpu.roll(ti, sh, 0)
            lr = lvl[2 * li]
            lim = lvl[2 * li + 1]
            tr, ti = tr + lr * sr - lim * si, ti + lr * si + lim * sr
        cbr = jnp.broadcast_to(c_re, tile)
        cbi = jnp.broadcast_to(c_im, tile)
        hr = tr + pw[0] * cbr - pw[1] * cbi
        hi = ti + pw[0] * cbi + pw[1] * cbr
        edge = 0 if rev else SUBLANES - 1
        c_re = hr[edge:edge + 1]
        c_im = hi[edge:edge + 1]
        h_re[i] = hr
        h_im[i] = hi
    carry[0:1, :] = c_re
    carry[1:2, :] = c_im
    y = (_dot(jnp.concatenate(h_re, axis=0), cre_ref[...])
         - _dot(jnp.concatenate(h_im, axis=0), cim_ref[...]))
    if readout:
        y = y + yf_ref[0] + d_ref[...] * u
        z = 0.5 * y * (1.0 + jnp.tanh(math.sqrt(2.0 / math.pi) * (y + 0.044715 * y * y * y)))
        y = z * _sigmoid(_dot(z, gw_ref[...]) + gb_ref[...])
    y_ref[0] = y

    @pl.when(j == n_chunks - 1)
    def _emit_state():
        hre_ref[0] = c_re
        him_ref[0] = c_im


def _s5_call(u, prm, d, h0, rev, readout, yf=None):
    b, l, _ = u.shape
    t = min(S5_BLOCK, l)
    n_chunks = l // t
    tok = pl.BlockSpec((1, t, S5_WIDTH), lambda i, j: (i, _chunk_index(j, n_chunks, rev), 0))
    row_n = pl.BlockSpec((1, S5_N), lambda i, j: (0, 0))
    st = pl.BlockSpec((1, 1, S5_N), lambda i, j: (i, 0, 0))

    def full(arr):
        return pl.BlockSpec(arr.shape, lambda i, j: (0,) * arr.ndim)

    args = [u, prm["a_re"][d], prm["a_im"][d], prm["log_dt"][d], prm["b_re"], prm["b_im"],
            prm["c_re"], prm["c_im"], h0[0], h0[1]]
    specs = [tok, row_n, row_n, row_n, full(prm["b_re"]), full(prm["b_im"]),
             full(prm["c_re"]), full(prm["c_im"]), st, st]
    if readout:
        args += [prm["d"], prm["glu_w"], prm["glu_b"], yf]
        specs += [full(prm["d"]), full(prm["glu_w"]), full(prm["glu_b"]), tok]
    y, hre, him = pl.pallas_call(
        functools.partial(_s5_kernel, rev=rev, readout=readout, n_chunks=n_chunks),
        grid=(b, n_chunks),
        in_specs=specs,
        out_specs=[tok, st, st],
        out_shape=[jax.ShapeDtypeStruct((b, l, S5_WIDTH), F32),
                   jax.ShapeDtypeStruct((b, 1, S5_N), F32),
                   jax.ShapeDtypeStruct((b, 1, S5_N), F32)],
        scratch_shapes=[pltpu.VMEM((S5_WIDTH, S5_N), BF16), pltpu.VMEM((S5_WIDTH, S5_N), BF16),
                        pltpu.VMEM((6, SUBLANES, S5_N), F32), pltpu.VMEM((2, SUBLANES, S5_N), F32),
                        pltpu.VMEM((2, S5_N), F32)],
        compiler_params=_cparams(("parallel", "arbitrary")),
        name="s5_bwd" if rev else "s5_fwd",
    )(*args)
    return y, (hre, him)


def _s5_mixer(u_ctx, u_lat, prm):
    b = u_lat.shape[0]
    zero = (jnp.zeros((b, 1, S5_N), F32), jnp.zeros((b, 1, S5_N), F32))
    yc_f, hc_f = _s5_call(u_ctx, prm, 0, zero, False, False)
    y_ctx, hc_b = _s5_call(u_ctx, prm, 1, zero, True, True, yc_f)
    yl_f, _ = _s5_call(u_lat, prm, 0, hc_f, False, False)
    y_lat, _ = _s5_call(u_lat, prm, 1, hc_b, True, True, yl_f)
    return y_ctx, y_lat


def _block_order(nc, rev):
    return range(nc - 1, -1, -1) if rev else range(nc)


def _store_heads(buf, ci, o):
    for h in range(HEADS):
        buf[ci * CHUNK:(ci + 1) * CHUNK, h * HEAD:(h + 1) * HEAD] = o[h]


def _gdn_kernel(*refs, rev, readout, n_blocks, d):
    pp_ref, pc_ref, pn_ref, cw_ref, alog_ref, dtb_ref, ones_ref, s0_ref = refs[:8]
    if readout:
        nw_ref, of_ref = refs[8:10]
        y_ref, so_ref = refs[10:12]
        state, obuf = refs[12:]
    else:
        y_ref, so_ref = refs[8:10]
        state, obuf = refs[10:]
    bt = pc_ref.shape[1]
    nc = bt // CHUNK
    j = pl.program_id(1)
    c = _chunk_index(j, n_blocks, rev)

    @pl.when(j == 0)
    def _load_state():
        state[...] = s0_ref[0]

    halo = pp_ref.shape[1]
    ext = jnp.concatenate([_halo_rows(pp_ref, c > 0), pc_ref[0],
                           _halo_rows(pn_ref, c < n_blocks - 1)], axis=0)
    incl = _tri_mask(rev, False)
    strict = _tri_mask(rev, True)
    sub = min(SUB_CHUNKS * CHUNK, bt)
    nc = sub // CHUNK
    cum = _chunk_cumsum_matrix(sub, rev)
    half = GDN_CONV // 2
    first_g = 2 * HEADS + d * HEADS
    edge = 0 if rev else CHUNK - 1

    def prepare(si):
        lo = si * sub
        seg = ext[lo:lo + sub + 2 * halo]
        x = seg[halo:halo + sub]
        qkv = seg[:, :3 * WIDTH]
        acc = cw_ref[half:half + 1, :] * x[:, :3 * WIDTH]
        for off in range(-half, half + 1):
            if off != 0:
                tap = pltpu.roll(qkv, (-off) % (sub + 2 * halo), 0)[halo:halo + sub]
                acc = acc + cw_ref[half + off:half + off + 1, :] * tap
                yield
        act = _silu(acc)
        q = act[:, :WIDTH]
        k = act[:, WIDTH:2 * WIDTH]
        v = act[:, 2 * WIDTH:]
        q2 = q * q
        k2 = k * k
        tail = x[:, 4 * WIDTH:]
        beta_all = _sigmoid(tail)
        g_all = -jnp.exp(alog_ref[...]) * _softplus(tail + dtb_ref[...])
        gated = nw_ref[...] * _silu(x[:, 3 * WIDTH:4 * WIDTH]) if readout else None
        yield
        yield
        q_ssq = _head_sum(q2, ones_ref)
        k_ssq = _head_sum(k2, ones_ref)
        gcs = _dot_f32(cum, g_all)
        yield
        yield
        q = q * (lax.rsqrt(q_ssq + EPS) * HEAD ** -0.5)
        k = k * lax.rsqrt(k_ssq + EPS)
        gcs_t = gcs.T
        gc = _split_cols(gcs, first_g)
        gr = jnp.stack([gcs_t[first_g + h:first_g + h + 1, ci * CHUNK:(ci + 1) * CHUNK]
                        for ci in range(nc) for h in range(HEADS)], axis=0)
        beta = _split_cols(beta_all, d * HEADS)
        yield
        decay = jnp.where(incl, jnp.exp(jnp.minimum(gc - gr, 0.0)), 0.0)
        yield
        qs = _split_heads(q)
        ks = _split_heads(k)
        yield
        vs = _split_heads(v)
        kb = ks * beta
        eg = jnp.exp(gc)
        g_last = gc[:, edge:edge + 1]
        yield
        return dict(lhs=jnp.concatenate([kb, qs], axis=1), ks=ks, decay=decay,
                    vw=jnp.concatenate([vs * beta, kb * eg], axis=2), q_eg=qs * eg,
                    kd=ks * jnp.exp(g_last - gc), dec_last=jnp.exp(g_last), gated=gated)

    def chain(si, op, s):
        rows = slice(si * sub, (si + 1) * sub)
        big = _bmm(op["lhs"], op["ks"], tb=True)
        yield
        m = jnp.where(strict, big[:, :CHUNK] * op["decay"], 0.0)
        attn = big[:, CHUNK:] * op["decay"]
        yield
        tinv = yield from _inv_unit_tri(m)
        uw = _bmm(tinv, op["vw"])
        yield
        au = _bmm(attn, uw)
        yield
        q_eff = op["q_eg"] - au[:, :, HEAD:]
        y_in = au[:, :, :HEAD]
        kdu = _bmm(op["kd"], uw, ta=True)
        yield
        first = si * nc
        for ci in _block_order(nc, rev):
            sl = slice(ci * HEADS, (ci + 1) * HEADS)
            _store_heads(obuf, first + ci, _bmm(q_eff[sl], s) + y_in[sl])
            s = s * op["dec_last"][sl] - _bmm(kdu[sl, :, HEAD:], s) + kdu[sl, :, :HEAD]
            yield
        o = obuf[rows]
        if readout:
            o = o + of_ref[0, rows]
            scale = lax.rsqrt(_head_sum(o * o, ones_ref) * (1.0 / HEAD) + EPS)
            o = o * scale * op["gated"]
        y_ref[0, rows] = o
        return s

    order = list(_block_order(bt // sub, rev))
    s = state[...]
    op = _run(prepare(order[0]))
    for n, si in enumerate(order):
        if n + 1 < len(order):
            s, op = _interleave(chain(si, op, s), prepare(order[n + 1]), CHAIN_LEAD)
        else:
            s = _run(chain(si, op, s))
    state[...] = s

    @pl.when(j == n_blocks - 1)
    def _emit_state():
        so_ref[0] = state[...]


def _block_rows(l, chunks):
    return min(chunks * CHUNK, l)


def _halo_specs(width, bt, n_blocks, rev, halo):
    per_block = bt // halo
    last = n_blocks * per_block - 1

    def cur(i, j):
        return (i, _chunk_index(j, n_blocks, rev), 0)

    def prev(i, j):
        return (i, jnp.maximum(_chunk_index(j, n_blocks, rev) * per_block - 1, 0), 0)

    def nxt(i, j):
        return (i, jnp.minimum((_chunk_index(j, n_blocks, rev) + 1) * per_block, last), 0)

    return [pl.BlockSpec((1, halo, width), prev), pl.BlockSpec((1, bt, width), cur),
            pl.BlockSpec((1, halo, width), nxt)]


def _gdn_call(p, prm, d, s0, rev, readout, of=None):
    b, l, _ = p.shape
    bt = _block_rows(l, BLOCK_CHUNKS)
    n_blocks = l // bt
    tok = pl.BlockSpec((1, bt, WIDTH), lambda i, j: (i, _chunk_index(j, n_blocks, rev), 0))
    st = pl.BlockSpec((1, HEADS, HEAD, HEAD), lambda i, j: (i, 0, 0, 0))

    def full(arr):
        return pl.BlockSpec(arr.shape, lambda i, j: (0,) * arr.ndim)

    args = [p, p, p, prm["conv"], prm["alog"][d], prm["dtb"][d], prm["ones"], s0]
    specs = _halo_specs(GDN_COLS_PACKED, bt, n_blocks, rev, SUBLANES) + [
        full(prm["conv"]), full(prm["alog"][d]), full(prm["dtb"][d]), full(prm["ones"]), st]
    if readout:
        args += [prm["norm"], of]
        specs += [full(prm["norm"]), tok]
    return pl.pallas_call(
        functools.partial(_gdn_kernel, rev=rev, readout=readout, n_blocks=n_blocks, d=d),
        grid=(b, n_blocks),
        in_specs=specs,
        out_specs=[tok, st],
        out_shape=[jax.ShapeDtypeStruct((b, l, WIDTH), F32),
                   jax.ShapeDtypeStruct((b, HEADS, HEAD, HEAD), F32)],
        scratch_shapes=[pltpu.VMEM((HEADS, HEAD, HEAD), F32), pltpu.VMEM((bt, WIDTH), F32)],
        compiler_params=_cparams(("parallel", "arbitrary")),
        name="gdn_bwd" if rev else "gdn_fwd",
    )(*args)


def _gdn_mixer(p_ctx, p_lat, prm):
    b = p_lat.shape[0]
    zero = jnp.zeros((b, HEADS, HEAD, HEAD), F32)
    oc_f, sc_f = _gdn_call(p_ctx, prm, 0, zero, False, False)
    y_ctx, sc_b = _gdn_call(p_ctx, prm, 1, zero, True, True, oc_f)
    ol_f, _ = _gdn_call(p_lat, prm, 0, sc_f, False, False)
    y_lat, _ = _gdn_call(p_lat, prm, 1, sc_b, True, True, ol_f)
    return y_ctx, y_lat


def _rwkv_kernel(*refs, rev, readout, grid_shift, n_blocks):
    (pp_ref, pc_ref, pn_ref, mu_ref, w0_ref, wup_ref, a0_ref, aup_ref, kk_ref, ka_ref, rk_ref,
     ones_ref, s0_ref) = refs[:13]
    if readout:
        gup_ref, lnw_ref, lnb_ref, yf_ref, bf_ref = refs[13:18]
        y_ref, so_ref = refs[18:20]
        state, obuf = refs[20:]
    else:
        y_ref, bon_ref, so_ref = refs[13:16]
        state, obuf = refs[16:]
    bt = pc_ref.shape[1]
    nc = bt // CHUNK
    j = pl.program_id(1)
    c = _chunk_index(j, n_blocks, rev)

    @pl.when(j == 0)
    def _load_state():
        state[...] = s0_ref[0]

    halo = pp_ref.shape[1]
    ext = jnp.concatenate([_halo_rows(pp_ref, c > 0), pc_ref[0],
                           _halo_rows(pn_ref, c < n_blocks - 1)], axis=0)
    incl = _tri_mask(rev, False)
    strict = _tri_mask(rev, True)
    sub = min(SUB_CHUNKS * CHUNK, bt)
    cum = _chunk_cumsum_matrix(sub, rev)

    def prepare(si):
        lo = si * sub
        seg = ext[lo:lo + sub + 2 * halo]
        x = seg[halo:halo + sub]
        lane = lax.broadcasted_iota(jnp.int32, x.shape, 1)
        if grid_shift:
            col = lax.broadcasted_iota(jnp.int32, x.shape, 0) & (GRID_W - 1)
            left = jnp.where(col == 0, 0.0, pltpu.roll(x, 1, 0))
            right = jnp.where(col == GRID_W - 1, 0.0, pltpu.roll(x, sub - 1, 0))
            cm = lane & 3
            shifted = jnp.where(cm == 0, left, jnp.where(
                cm == 1, right, jnp.where(cm == 2, seg[:sub], seg[2 * halo:])))
        else:
            shifted = jnp.where((lane & 1) == 0, pltpu.roll(seg, 1, 0)[halo:halo + sub],
                                pltpu.roll(seg, sub + 2 * halo - 1, 0)[halo:halo + sub])
        yield
        pm = x + (shifted - x) * mu_ref[...]
        r = pm[:, :WIDTH]
        k = pm[:, WIDTH:2 * WIDTH]
        v = pm[:, 2 * WIDTH:3 * WIDTH]
        o1 = 3 * WIDTH
        w_dn = pm[:, o1:o1 + 2 * LORA]
        a_dn = pm[:, o1 + 2 * LORA:o1 + 4 * LORA]
        g_dn = pm[:, o1 + 4 * LORA:]
        w_act = jnp.tanh(w_dn)
        g_act = _sigmoid(g_dn)
        kkr = k * kk_ref[...]
        kkr2 = kkr * kkr
        yield
        yield
        w_lora = _dot(w_act, wup_ref[...])
        a_lora = _dot(a_dn, aup_ref[...])
        kk_ssq = _head_sum(kkr2, ones_ref)
        gate = _dot(g_act, gup_ref[...]) if readout else None
        yield
        yield
        w_log = -_softplus(-(w0_ref[...] + w_lora)) - 0.5
        lw = -jnp.exp(w_log)
        iclr = _sigmoid(a0_ref[...] + a_lora)
        kk = kkr * lax.rsqrt(kk_ssq + EPS)
        k_dir = k * (1.0 + (iclr - 1.0) * ka_ref[...])
        b_dir = kk * iclr
        rkr = r * k_dir * rk_ref[...]
        yield
        yield
        g_inc = _dot_f32(cum, lw)
        bonus = _head_sum(rkr, ones_ref) * v
        yield
        yield
        g_end = _chunk_edge_rows(g_inc, rev)
        e_neg = jnp.exp(-g_inc)
        e_end = jnp.exp(g_end - g_inc)
        yield
        lhs = jnp.concatenate([_split_heads(kk * jnp.exp(g_inc - lw)),
                               _split_heads(r * jnp.exp(g_inc))], axis=1)
        yield
        rhs = jnp.concatenate([_split_heads(b_dir * e_neg), _split_heads(k_dir * e_neg)], axis=1)
        yield
        b_eh = _split_heads(b_dir * e_end)
        yield
        k_eh = _split_heads(k_dir * e_end)
        yield
        v_h = _split_heads(v)
        dec_end = _split_heads(jnp.exp(g_end))[:, :1, :]
        return dict(lhs=lhs, rhs=rhs, b_eh=b_eh, k_eh=k_eh, v_h=v_h, dec_end=dec_end,
                    bonus=bonus, gate=gate)

    def chain(si, op, s):
        rows = slice(si * sub, (si + 1) * sub)
        kap_h = op["lhs"][:, :CHUNK]
        r_h = op["lhs"][:, CHUNK:]
        v_h = op["v_h"]
        big = _bmm(op["lhs"], op["rhs"], tb=True)
        yield
        a_b = jnp.where(strict, big[:, :CHUNK, :CHUNK], 0.0)
        a_k = jnp.where(strict, big[:, :CHUNK, CHUNK:], 0.0)
        r_b = jnp.where(incl, big[:, CHUNK:, :CHUNK], 0.0)
        r_k = jnp.where(incl, big[:, CHUNK:, CHUNK:], 0.0)
        akv = _bmm(a_k, v_h)
        yield
        tinv = yield from _inv_unit_tri(a_b)
        z = _bmm(tinv, jnp.concatenate([kap_h, akv], axis=2))
        yield
        rz = _bmm(r_b, z)
        yield
        q_eff = r_h - rz[:, :, :HEAD]
        y_in = _bmm(r_k, v_h) - rz[:, :, HEAD:]
        yield
        ztb = _bmm(z, op["b_eh"], ta=True)
        yield
        pt = ztb[:, :HEAD]
        xt = _bmm(v_h, op["k_eh"], ta=True) - ztb[:, HEAD:]
        yield
        first = si * (sub // CHUNK)
        for ci in _block_order(sub // CHUNK, rev):
            sl = slice(ci * HEADS, (ci + 1) * HEADS)
            _store_heads(obuf, first + ci, _bmm(q_eff[sl], s, tb=True) + y_in[sl])
            s = s * op["dec_end"][sl] - _bmm(s, pt[sl]) + xt[sl]
            yield
        y = obuf[rows]
        if readout:
            y = y + yf_ref[0, rows]
            yc = y - _head_sum(y, ones_ref) * (1.0 / HEAD)
            var = _head_sum(yc * yc, ones_ref) * (1.0 / HEAD)
            yn = yc * lax.rsqrt(var + RWKV_GN_EPS) * lnw_ref[...] + lnb_ref[...]
            y_ref[0, rows] = (yn + op["bonus"] + bf_ref[0, rows]) * op["gate"]
        else:
            y_ref[0, rows] = y
            bon_ref[0, rows] = op["bonus"]
        return s

    order = list(_block_order(bt // sub, rev))
    s = state[...]
    op = _run(prepare(order[0]))
    for n, si in enumerate(order):
        if n + 1 < len(order):
            s, op = _interleave(chain(si, op, s), prepare(order[n + 1]), CHAIN_LEAD)
        else:
            s = _run(chain(si, op, s))
    state[...] = s

    @pl.when(j == n_blocks - 1)
    def _emit_state():
        so_ref[0] = state[...]


def _rwkv_call(p, prm, d, s0, rev, readout, grid_shift, yf=None, bf=None):
    b, l, _ = p.shape
    bt = _block_rows(l, BLOCK_CHUNKS)
    n_blocks = l // bt
    tok = pl.BlockSpec((1, bt, WIDTH), lambda i, j: (i, _chunk_index(j, n_blocks, rev), 0))
    st = pl.BlockSpec((1, HEADS, HEAD, HEAD), lambda i, j: (i, 0, 0, 0))

    def full(arr):
        return pl.BlockSpec(arr.shape, lambda i, j: (0,) * arr.ndim)

    small = [prm["mu"], prm["w0"][d], prm["w_up"][d], prm["a0"][d], prm["a_up"][d],
             prm["k_k"], prm["k_a"], prm["r_k"], prm["ones"]]
    args = [p, p, p] + small + [s0]
    halo = GRID_W if grid_shift else SUBLANES
    specs = _halo_specs(RWKV_COLS, bt, n_blocks, rev, halo) + [full(a) for a in small] + [st]
    tok_shape = jax.ShapeDtypeStruct((b, l, WIDTH), F32)
    st_shape = jax.ShapeDtypeStruct((b, HEADS, HEAD, HEAD), F32)
    if readout:
        extra = [prm["g_up"], prm["ln_w"], prm["ln_b"]]
        args += extra + [yf, bf]
        specs += [full(a) for a in extra] + [tok, tok]
        out_specs, out_shape = [tok, st], [tok_shape, st_shape]
    else:
        out_specs, out_shape = [tok, tok, st], [tok_shape, tok_shape, st_shape]
    return pl.pallas_call(
        functools.partial(_rwkv_kernel, rev=rev, readout=readout, grid_shift=grid_shift,
                          n_blocks=n_blocks),
        grid=(b, n_blocks),
        in_specs=specs,
        out_specs=out_specs,
        out_shape=out_shape,
        scratch_shapes=[pltpu.VMEM((HEADS, HEAD, HEAD), F32), pltpu.VMEM((bt, WIDTH), F32)],
        compiler_params=_cparams(("parallel", "arbitrary")),
        name="rwkv_bwd" if rev else "rwkv_fwd",
    )(*args)


def _rwkv_mixer(p_ctx, p_lat, prm):
    b = p_lat.shape[0]
    zero = jnp.zeros((b, HEADS, HEAD, HEAD), F32)
    yc_f, bc_f, sc_f = _rwkv_call(p_ctx, prm, 0, zero, False, False, False)
    y_ctx, sc_b = _rwkv_call(p_ctx, prm, 1, zero, True, True, False, yc_f, bc_f)
    yl_f, bl_f, _ = _rwkv_call(p_lat, prm, 0, sc_f, False, False, True)
    y_lat, _ = _rwkv_call(p_lat, prm, 1, sc_b, True, True, True, yl_f, bl_f)
    return y_ctx, y_lat


def _pack_w_in(w):
    o_gdn = S5_WIDTH
    o_tail = o_gdn + 4 * WIDTH
    o_rwkv = o_tail + 4 * HEADS
    pad = jnp.zeros((w.shape[0], GDN_TAIL - 4 * HEADS), w.dtype)
    return jnp.concatenate([w[:, :o_tail], w[:, o_tail:o_rwkv], pad, w[:, o_rwkv:]], axis=1)


def _block_diag(blocks):
    g, r, c = blocks.shape
    eye = jnp.eye(g, dtype=blocks.dtype)
    return (eye[:, None, :, None] * blocks[:, :, None, :]).reshape(g * r, g * c)


def _s5_params(layer, b_re, b_im, c_re, c_im, d, a_re, a_im, log_dt, glu_w, glu_b):
    row = lambda v: v.reshape(2, 1, S5_N)
    return {
        "a_re": row(a_re[layer]), "a_im": row(a_im[layer]),
        "log_dt": row(jnp.repeat(log_dt[layer], S5_STATE, axis=-1)),
        "b_re": _block_diag(jnp.swapaxes(b_re[layer], 1, 2)),
        "b_im": _block_diag(jnp.swapaxes(b_im[layer], 1, 2)),
        "c_re": _block_diag(jnp.swapaxes(c_re[layer], 1, 2)),
        "c_im": _block_diag(jnp.swapaxes(c_im[layer], 1, 2)),
        "d": d[layer].reshape(1, S5_WIDTH), "glu_w": glu_w[layer],
        "glu_b": glu_b[layer].reshape(1, S5_WIDTH),
    }


def _head_ones():
    return _block_diag(jnp.ones((HEADS, HEAD, HEAD), BF16))


def _gdn_params(layer, conv, a_log, dt_bias, norm):
    def tail_row(v):
        rows = []
        for d in range(2):
            z = jnp.zeros((GDN_TAIL,), F32)
            rows.append(z.at[2 * HEADS + d * HEADS:2 * HEADS + (d + 1) * HEADS].set(v[d])
                        .reshape(1, GDN_TAIL))
        return rows

    return {"conv": conv[layer], "alog": tail_row(a_log[layer]), "dtb": tail_row(dt_bias[layer]),
            "norm": jnp.tile(norm[layer], HEADS).reshape(1, WIDTH), "ones": _head_ones()}


def _rwkv_params(layer, mu, w0, w_up, a0, a_up, g_up, k_k, k_a, r_k, ln_w, ln_b):
    def lora_pad(w):
        z = jnp.zeros((LORA, WIDTH), F32)
        return [jnp.concatenate([w[0], z], axis=0), jnp.concatenate([z, w[1]], axis=0)]

    row = lambda v: v.reshape(1, -1)
    return {
        "mu": row(mu[layer]), "w0": [row(w0[layer, 0]), row(w0[layer, 1])],
        "w_up": lora_pad(w_up[layer]), "a0": [row(a0[layer, 0]), row(a0[layer, 1])],
        "a_up": lora_pad(a_up[layer]), "g_up": g_up[layer], "k_k": row(k_k[layer]),
        "k_a": row(k_a[layer]), "r_k": row(r_k[layer]), "ln_w": row(ln_w[layer]),
        "ln_b": row(ln_b[layer]), "ones": _head_ones(),
    }


def kernel(x, c, ctx, c_ctx, mod_w, mod_b, norm_mix, norm_mlp, norm_final, w_in, w_out, s5_b_re, s5_b_im, s5_c_re, s5_c_im, s5_d, s5_a_re, s5_a_im, s5_log_dt, s5_glu_w, s5_glu_b, gdn_conv, gdn_a_log, gdn_dt_bias, gdn_norm, rwkv_mu, rwkv_w0, rwkv_w_up, rwkv_a0, rwkv_a_up, rwkv_g_up, rwkv_k_k, rwkv_k_a, rwkv_r_k, rwkv_ln_w, rwkv_ln_b, mlp_w1, mlp_w2):
    depth = mod_w.shape[0]
    batch = x.shape[0]
    x = x.astype(F32)
    ctx = ctx.astype(F32)
    rows = SUBLANES * ((batch + 1 + SUBLANES - 1) // SUBLANES)
    cvec = jnp.zeros((rows, D_MODEL), F32).at[:batch].set(c.astype(F32)).at[batch].set(
        c_ctx.astype(F32))
    mods_all = _mods(cvec, mod_w, mod_b)

    for layer in range(depth):
        ctx_out = layer < depth - 1
        m = mods_all[layer].reshape(rows, N_MOD, 1, D_MODEL)
        lat = [m[:batch, i] for i in range(N_MOD)]
        cm = [jnp.broadcast_to(m[batch, i][None], (batch, 1, D_MODEL)) for i in range(N_MOD)]
        w_packed = _pack_w_in(w_in[layer]).astype(BF16)
        wo = w_out[layer].astype(BF16)
        wo_parts = (wo[:S5_WIDTH], wo[S5_WIDTH:S5_WIDTH + WIDTH], wo[S5_WIDTH + WIDTH:])
        w1 = mlp_w1[layer].astype(BF16)
        w2 = mlp_w2[layer].astype(BF16)

        pa_l, pb_l, pc_l = _inproj(x, norm_mix[layer], lat[1], lat[0], w_packed)
        pa_c, pb_c, pc_c = _inproj(ctx, norm_mix[layer], cm[1], cm[0], w_packed)

        ya_c, ya_l = _s5_mixer(pa_c, pa_l, _s5_params(
            layer, s5_b_re, s5_b_im, s5_c_re, s5_c_im, s5_d, s5_a_re, s5_a_im, s5_log_dt,
            s5_glu_w, s5_glu_b))
        yb_c, yb_l = _gdn_mixer(pb_c, pb_l, _gdn_params(
            layer, gdn_conv, gdn_a_log, gdn_dt_bias, gdn_norm))
        yc_c, yc_l = _rwkv_mixer(pc_c, pc_l, _rwkv_params(
            layer, rwkv_mu, rwkv_w0, rwkv_w_up, rwkv_a0, rwkv_a_up, rwkv_g_up, rwkv_k_k,
            rwkv_k_a, rwkv_r_k, rwkv_ln_w, rwkv_ln_b))

        x = _outproj(x, ya_l, yb_l, yc_l, wo_parts, lat[2], norm_mlp[layer], lat[4], lat[3],
                     lat[5], w1, w2, norm_final, final=not ctx_out)
        if ctx_out:
            ctx = _outproj(ctx, ya_c, yb_c, yc_c, wo_parts, cm[2], norm_mlp[layer], cm[4], cm[3],
                           cm[5], w1, w2, norm_final, final=False)
    return x
```

```python
import functools
import math

import jax
import jax.numpy as jnp
from jax import lax
from jax.experimental import pallas as pl
from jax.experimental.pallas import tpu as pltpu

F32 = jnp.float32
BF16 = jnp.bfloat16
HIGHEST = lax.Precision.HIGHEST

D_MODEL = 1024
D_FF = 4 * D_MODEL
N_MOD = 6
EPS = 1e-6
GRID_W = 64

S5_WIDTH = 256
S5_GROUP = 16
S5_GROUPS = 16
S5_STATE = 64
S5_N = S5_GROUPS * S5_STATE
S5_BLOCK = 512

HEAD = 64
HEADS = 6
WIDTH = HEADS * HEAD
GDN_CONV = 5
GDN_TAIL = 128
GDN_COLS_PACKED = 4 * WIDTH + GDN_TAIL
LORA = 64
GATE_LORA = 128
RWKV_COLS = 3 * WIDTH + 2 * LORA + 2 * LORA + GATE_LORA
RWKV_GN_EPS = 64e-5
CHUNK = 64
SUBLANES = 8
BLOCK_CHUNKS = 16
SUB_CHUNKS = 4
CHAIN_LEAD = 2

ROW_TILE = 512


def _dot(a, b, dims=((1,), (0,))):
    return lax.dot_general(a.astype(BF16), b.astype(BF16), (dims, ((), ())),
                           preferred_element_type=F32)


def _dot_f32(a, b, dims=((1,), (0,))):
    return lax.dot_general(a, b, (dims, ((), ())), precision=HIGHEST,
                           preferred_element_type=F32)


_NT = ((1,), (1,))
_TN = ((0,), (0,))


def _sigmoid(x):
    return 0.5 + 0.5 * jnp.tanh(0.5 * x)


def _silu(x):
    h = 0.5 * x
    return h + h * jnp.tanh(h)


def _softplus(x):
    return jnp.maximum(x, 0.0) + jnp.log1p(jnp.exp(-jnp.abs(x)))


def _bmm(a, b, ta=False, tb=False):
    dims = (((1 if ta else 2,), (2 if tb else 1,)), ((0,), (0,)))
    return lax.dot_general(a.astype(BF16), b.astype(BF16), dims, preferred_element_type=F32)


def _tri_mask(rev, strict):
    shape = (1, CHUNK, CHUNK)
    r = lax.broadcasted_iota(jnp.int32, shape, 1)
    c = lax.broadcasted_iota(jnp.int32, shape, 2)
    if rev:
        return (c > r) if strict else (c >= r)
    return (c < r) if strict else (c <= r)


def _chunk_cumsum_matrix(n, rev):
    r = lax.broadcasted_iota(jnp.int32, (n, n), 0)
    c = lax.broadcasted_iota(jnp.int32, (n, n), 1)
    shift = int(math.log2(CHUNK))
    same = (r >> shift) == (c >> shift)
    return jnp.where(same & ((c >= r) if rev else (c <= r)), 1.0, 0.0).astype(F32)


def _inv_unit_tri(a):
    n = a.shape[-1]
    r = lax.broadcasted_iota(jnp.int32, (1, n, n), 1)
    c = lax.broadcasted_iota(jnp.int32, (1, n, n), 2)
    inv = jnp.where(r == c, 1.0, 0.0).astype(F32) - jnp.where((r >> 1) == (c >> 1), a, 0.0)
    for lvl in range(1, int(math.log2(n))):
        couple = ((r >> (lvl + 1)) == (c >> (lvl + 1))) & ((r >> lvl) != (c >> lvl))
        half = _bmm(inv, jnp.where(couple, a, 0.0))
        yield
        inv = inv - _bmm(half, inv)
        yield
    return inv


def _run(gen):
    while True:
        try:
            next(gen)
        except StopIteration as stop:
            return stop.value


def _interleave(main, side, lead):
    results = {}
    live = {"main": main, "side": side}
    turn = 0
    while live:
        for name in ("main", "side"):
            if name in live and not (name == "side" and turn < lead and "main" in live):
                try:
                    next(live[name])
                except StopIteration as stop:
                    results[name] = stop.value
                    del live[name]
        turn += 1
    return results["main"], results["side"]


def _split_heads(x):
    nc = x.shape[0] // CHUNK
    return jnp.stack([x[ci * CHUNK:(ci + 1) * CHUNK, h * HEAD:(h + 1) * HEAD]
                      for ci in range(nc) for h in range(HEADS)], axis=0)


def _split_cols(x, first):
    nc = x.shape[0] // CHUNK
    return jnp.stack([x[ci * CHUNK:(ci + 1) * CHUNK, first + h:first + h + 1]
                      for ci in range(nc) for h in range(HEADS)], axis=0)


def _chunk_edge_rows(x, rev):
    nc = x.shape[0] // CHUNK
    edge = 0 if rev else CHUNK - 1
    return jnp.concatenate(
        [jnp.broadcast_to(x[ci * CHUNK + edge:ci * CHUNK + edge + 1], (CHUNK, x.shape[1]))
         for ci in range(nc)], axis=0)


def _head_sum(x, ones_ref):
    return _dot(x, ones_ref[...])


def _halo_rows(ref, keep):
    return jnp.where(keep, ref[0], 0.0)


def _chunk_index(j, n_chunks, rev):
    return (n_chunks - 1 - j) if rev else j


def _cparams(sem):
    return pltpu.CompilerParams(dimension_semantics=sem)


def _mods_kernel(c_ref, w_ref, b_ref, o_ref):
    o_ref[0] = _dot_f32(_silu(c_ref[...]), w_ref[0]) + b_ref[0]


def _mods(cvec, mod_w, mod_b):
    depth = mod_w.shape[0]
    rows = cvec.shape[0]
    n_tile = 1536
    return pl.pallas_call(
        _mods_kernel,
        grid=(depth, N_MOD * D_MODEL // n_tile),
        in_specs=[
            pl.BlockSpec((rows, D_MODEL), lambda l, n: (0, 0)),
            pl.BlockSpec((1, D_MODEL, n_tile), lambda l, n: (l, 0, n)),
            pl.BlockSpec((1, 1, n_tile), lambda l, n: (l, 0, n)),
        ],
        out_specs=pl.BlockSpec((1, rows, n_tile), lambda l, n: (l, 0, n)),
        out_shape=jax.ShapeDtypeStruct((depth, rows, N_MOD * D_MODEL), F32),
        compiler_params=_cparams(("arbitrary", "arbitrary")),
        name="mods",
    )(cvec, mod_w, mod_b.reshape(depth, 1, N_MOD * D_MODEL))


def _inproj_kernel(x_ref, g_ref, sc_ref, sh_ref, w_ref, oa_ref, ob_ref, oc_ref):
    x = x_ref[0]
    xn = x * lax.rsqrt(jnp.mean(x * x, axis=-1, keepdims=True) + EPS) * g_ref[...]
    xm = xn * (1.0 + sc_ref[0]) + sh_ref[0]
    p = _dot(xm, w_ref[...])
    oa_ref[0] = p[:, :S5_WIDTH]
    ob_ref[0] = p[:, S5_WIDTH:S5_WIDTH + GDN_COLS_PACKED]
    oc_ref[0] = p[:, S5_WIDTH + GDN_COLS_PACKED:]


def _inproj(x, gain, sc, sh, w_packed):
    b, l, _ = x.shape
    tm = min(ROW_TILE, l)
    cols = w_packed.shape[1]
    vec = pl.BlockSpec((1, 1, D_MODEL), lambda i, t: (i, 0, 0))
    return pl.pallas_call(
        _inproj_kernel,
        grid=(b, l // tm),
        in_specs=[
            pl.BlockSpec((1, tm, D_MODEL), lambda i, t: (i, t, 0)),
            pl.BlockSpec((1, D_MODEL), lambda i, t: (0, 0)),
            vec, vec,
            pl.BlockSpec((D_MODEL, cols), lambda i, t: (0, 0), pipeline_mode=pl.Buffered(1)),
        ],
        out_specs=[
            pl.BlockSpec((1, tm, S5_WIDTH), lambda i, t: (i, t, 0)),
            pl.BlockSpec((1, tm, GDN_COLS_PACKED), lambda i, t: (i, t, 0)),
            pl.BlockSpec((1, tm, RWKV_COLS), lambda i, t: (i, t, 0)),
        ],
        out_shape=[
            jax.ShapeDtypeStruct((b, l, S5_WIDTH), F32),
            jax.ShapeDtypeStruct((b, l, GDN_COLS_PACKED), F32),
            jax.ShapeDtypeStruct((b, l, RWKV_COLS), F32),
        ],
        compiler_params=_cparams(("parallel", "arbitrary")),
        name="inproj",
    )(x, gain.reshape(1, D_MODEL), sc, sh, w_packed)


def _outproj_kernel(x_ref, ya_ref, yb_ref, yc_ref, woa_ref, wob_ref, woc_ref, gt1_ref,
                    g_ref, sc_ref, sh_ref, gt2_ref, w1_ref, w2_ref, gf_ref, o_ref, *, final):
    mix = (_dot(ya_ref[0], woa_ref[...]) + _dot(yb_ref[0], wob_ref[...])
           + _dot(yc_ref[0], woc_ref[...]))
    x1 = x_ref[0] + gt1_ref[0] * mix
    h = x1 * lax.rsqrt(jnp.mean(x1 * x1, axis=-1, keepdims=True) + EPS) * g_ref[...]
    h = h * (1.0 + sc_ref[0]) + sh_ref[0]
    a = jnp.maximum(_dot(h, w1_ref[...]), 0.0)
    x2 = x1 + gt2_ref[0] * _dot(a * a, w2_ref[...])
    if final:
        x2 = x2 * lax.rsqrt(jnp.mean(x2 * x2, axis=-1, keepdims=True) + EPS) * gf_ref[...]
    o_ref[0] = x2


def _outproj(x, ya, yb, yc, wo_parts, gt1, gain, sc, sh, gt2, w1, w2, gain_final, final):
    b, l, _ = x.shape
    tm = min(ROW_TILE, l)
    vec = pl.BlockSpec((1, 1, D_MODEL), lambda i, t: (i, 0, 0))
    row = pl.BlockSpec((1, D_MODEL), lambda i, t: (0, 0))

    def tok(width):
        return pl.BlockSpec((1, tm, width), lambda i, t: (i, t, 0))

    def full(arr):
        return pl.BlockSpec(arr.shape, lambda i, t: (0, 0), pipeline_mode=pl.Buffered(1))

    woa, wob, woc = wo_parts
    return pl.pallas_call(
        functools.partial(_outproj_kernel, final=final),
        grid=(b, l // tm),
        in_specs=[tok(D_MODEL), tok(S5_WIDTH), tok(WIDTH), tok(WIDTH),
                  full(woa), full(wob), full(woc), vec, row, vec, vec, vec,
                  full(w1), full(w2), row],
        out_specs=tok(D_MODEL),
        out_shape=jax.ShapeDtypeStruct((b, l, D_MODEL), F32),
        compiler_params=_cparams(("parallel", "arbitrary")),
        name="outproj_mlp",
    )(x, ya, yb, yc, woa, wob, woc, gt1, gain.reshape(1, D_MODEL), sc, sh, gt2, w1, w2,
      gain_final.reshape(1, D_MODEL))


def _cmul(a, b):
    return a[0] * b[0] - a[1] * b[1], a[0] * b[1] + a[1] * b[0]


def _s5_kernel(*refs, rev, readout, n_chunks):
    (u_ref, are_ref, aim_ref, ldt_ref, bre_ref, bim_ref, cre_ref, cim_ref,
     h0re_ref, h0im_ref) = refs[:10]
    if readout:
        d_ref, gw_ref, gb_ref, yf_ref = refs[10:14]
        y_ref, hre_ref, him_ref = refs[14:17]
        wre, wim, abt, pw, carry, perm_ref = refs[17:]
    else:
        y_ref, hre_ref, him_ref = refs[10:13]
        wre, wim, abt, pw, carry, perm_ref = refs[13:]
    t = u_ref.shape[1]
    seg = t // SUBLANES
    j = pl.program_id(1)
    tile = (SUBLANES, S5_N)

    @pl.when(j == 0)
    def _discretise():
        dt = jnp.exp(ldt_ref[...])
        a_re = are_ref[...]
        a_im = aim_ref[...]
        mag = jnp.exp(a_re * dt)
        ab = (mag * jnp.cos(a_im * dt), mag * jnp.sin(a_im * dt))
        den = a_re * a_re + a_im * a_im
        xr = ab[0] - 1.0
        cr = (xr * a_re + ab[1] * a_im) / den
        ci = (ab[1] * a_re - xr * a_im) / den
        wre[...] = (bre_ref[...] * cr - bim_ref[...] * ci).astype(BF16)
        wim[...] = (bre_ref[...] * ci + bim_ref[...] * cr).astype(BF16)
        abt[0] = jnp.broadcast_to(ab[0], tile)
        abt[1] = jnp.broadcast_to(ab[1], tile)
        power = ab
        for k in range(seg):
            pw[0, k] = jnp.broadcast_to(power[0], tile)
            pw[1, k] = jnp.broadcast_to(power[1], tile)
            if k + 1 < seg:
                power = _cmul(power, ab)
        carry[0:1, :] = h0re_ref[0]
        carry[1:2, :] = h0im_ref[0]

        m_idx = lax.broadcasted_iota(jnp.int32, (t, t), 0)
        n_idx = lax.broadcasted_iota(jnp.int32, (t, t), 1)
        src = seg * (m_idx & (SUBLANES - 1)) + (m_idx >> int(math.log2(SUBLANES)))
        perm_ref[...] = jnp.where(n_idx == src, 1.0, 0.0).astype(BF16)

    u = u_ref[0]
    perm = perm_ref[...]
    u_p = _dot(perm, u)
    bur = _dot(u_p, wre[...])
    bui = _dot(u_p, wim[...])
    steps = range(seg - 1, -1, -1) if rev else range(seg)
    a_r = abt[0]
    a_i = abt[1]
    hr = jnp.zeros(tile, F32)
    hi = jnp.zeros(tile, F32)
    local = [None] * seg
    for k in steps:
        rows = slice(k * SUBLANES, (k + 1) * SUBLANES)
        hr, hi = a_r * hr - a_i * hi + bur[rows], a_r * hi + a_i * hr + bui[rows]
        local[k] = (hr, hi)
    a_seg = (pw[0, seg - 1][0:1], pw[1, seg - 1][0:1])
    c = (carry[0:1, :], carry[1:2, :])
    row = lax.broadcasted_iota(jnp.int32, tile, 0)
    c_re = jnp.zeros(tile, F32)
    c_im = jnp.zeros(tile, F32)
    for r in (range(SUBLANES - 1, -1, -1) if rev else range(SUBLANES)):
        c_re = jnp.where(row == r, c[0], c_re)
        c_im = jnp.where(row == r, c[1], c_im)
        c = (hr[r:r + 1] + a_seg[0] * c[0] - a_seg[1] * c[1],
             hi[r:r + 1] + a_seg[0] * c[1] + a_seg[1] * c[0])
    carry[0:1, :] = c[0]
    carry[1:2, :] = c[1]
    h_re = [None] * seg
    h_im = [None] * seg
    for n, k in enumerate(steps):
        h_re[k] = local[k][0] + pw[0, n] * c_re - pw[1, n] * c_im
        h_im[k] = local[k][1] + pw[0, n] * c_im + pw[1, n] * c_re
    y_p = (_dot(jnp.concatenate(h_re, axis=0), cre_ref[...])
           - _dot(jnp.concatenate(h_im, axis=0), cim_ref[...]))
    y_hi = y_p.astype(BF16)
    y = _dot(perm, y_hi, _TN) + _dot(perm, y_p - y_hi.astype(F32), _TN)
    c_re, c_im = c
    if readout:
        y = y + yf_ref[0] + d_ref[...] * u
        z = 0.5 * y * (1.0 + jnp.tanh(math.sqrt(2.0 / math.pi) * (y + 0.044715 * y * y * y)))
        y = z * _sigmoid(_dot(z, gw_ref[...]) + gb_ref[...])
    y_ref[0] = y

    @pl.when(j == n_chunks - 1)
    def _emit_state():
        hre_ref[0] = c_re
        him_ref[0] = c_im


def _s5_call(u, prm, d, h0, rev, readout, yf=None):
    b, l, _ = u.shape
    t = min(S5_BLOCK, l)
    n_chunks = l // t
    tok = pl.BlockSpec((1, t, S5_WIDTH), lambda i, j: (i, _chunk_index(j, n_chunks, rev), 0))
    row_n = pl.BlockSpec((1, S5_N), lambda i, j: (0, 0))
    st = pl.BlockSpec((1, 1, S5_N), lambda i, j: (i, 0, 0))

    def full(arr):
        return pl.BlockSpec(arr.shape, lambda i, j: (0,) * arr.ndim)

    args = [u, prm["a_re"][d], prm["a_im"][d], prm["log_dt"][d], prm["b_re"], prm["b_im"],
            prm["c_re"], prm["c_im"], h0[0], h0[1]]
    specs = [tok, row_n, row_n, row_n, full(prm["b_re"]), full(prm["b_im"]),
             full(prm["c_re"]), full(prm["c_im"]), st, st]
    if readout:
        args += [prm["d"], prm["glu_w"], prm["glu_b"], yf]
        specs += [full(prm["d"]), full(prm["glu_w"]), full(prm["glu_b"]), tok]
    y, hre, him = pl.pallas_call(
        functools.partial(_s5_kernel, rev=rev, readout=readout, n_chunks=n_chunks),
        grid=(b, n_chunks),
        in_specs=specs,
        out_specs=[tok, st, st],
        out_shape=[jax.ShapeDtypeStruct((b, l, S5_WIDTH), F32),
                   jax.ShapeDtypeStruct((b, 1, S5_N), F32),
                   jax.ShapeDtypeStruct((b, 1, S5_N), F32)],
        scratch_shapes=[pltpu.VMEM((S5_WIDTH, S5_N), BF16), pltpu.VMEM((S5_WIDTH, S5_N), BF16),
                        pltpu.VMEM((2, SUBLANES, S5_N), F32),
                        pltpu.VMEM((2, t // SUBLANES, SUBLANES, S5_N), F32),
                        pltpu.VMEM((2, S5_N), F32), pltpu.VMEM((t, t), BF16)],
        compiler_params=_cparams(("parallel", "arbitrary")),
        name="s5_bwd" if rev else "s5_fwd",
    )(*args)
    return y, (hre, him)


def _s5_mixer(u_ctx, u_lat, prm):
    b = u_lat.shape[0]
    zero = (jnp.zeros((b, 1, S5_N), F32), jnp.zeros((b, 1, S5_N), F32))
    yc_f, hc_f = _s5_call(u_ctx, prm, 0, zero, False, False)
    y_ctx, hc_b = _s5_call(u_ctx, prm, 1, zero, True, True, yc_f)
    yl_f, _ = _s5_call(u_lat, prm, 0, hc_f, False, False)
    y_lat, _ = _s5_call(u_lat, prm, 1, hc_b, True, True, yl_f)
    return y_ctx, y_lat


def _block_order(nc, rev):
    return range(nc - 1, -1, -1) if rev else range(nc)


def _store_heads(buf, ci, o):
    for h in range(HEADS):
        buf[ci * CHUNK:(ci + 1) * CHUNK, h * HEAD:(h + 1) * HEAD] = o[h]


def _gdn_kernel(*refs, rev, readout, n_blocks, d):
    pp_ref, pc_ref, pn_ref, cw_ref, alog_ref, dtb_ref, ones_ref, s0_ref = refs[:8]
    if readout:
        nw_ref, of_ref = refs[8:10]
        y_ref, so_ref = refs[10:12]
        state, obuf = refs[12:]
    else:
        y_ref, so_ref = refs[8:10]
        state, obuf = refs[10:]
    bt = pc_ref.shape[1]
    nc = bt // CHUNK
    j = pl.program_id(1)
    c = _chunk_index(j, n_blocks, rev)

    @pl.when(j == 0)
    def _load_state():
        state[...] = s0_ref[0]

    halo = pp_ref.shape[1]
    ext = jnp.concatenate([_halo_rows(pp_ref, c > 0), pc_ref[0],
                           _halo_rows(pn_ref, c < n_blocks - 1)], axis=0)
    incl = _tri_mask(rev, False)
    strict = _tri_mask(rev, True)
    sub = min(SUB_CHUNKS * CHUNK, bt)
    nc = sub // CHUNK
    cum = _chunk_cumsum_matrix(sub, rev)
    half = GDN_CONV // 2
    first_g = 2 * HEADS + d * HEADS
    edge = 0 if rev else CHUNK - 1

    def prepare(si):
        lo = si * sub
        seg = ext[lo:lo + sub + 2 * halo]
        x = seg[halo:halo + sub]
        qkv = seg[:, :3 * WIDTH]
        acc = cw_ref[half:half + 1, :] * x[:, :3 * WIDTH]
        for off in range(-half, half + 1):
            if off != 0:
                tap = pltpu.roll(qkv, (-off) % (sub + 2 * halo), 0)[halo:halo + sub]
                acc = acc + cw_ref[half + off:half + off + 1, :] * tap
                yield
        act = _silu(acc)
        q = act[:, :WIDTH]
        k = act[:, WIDTH:2 * WIDTH]
        v = act[:, 2 * WIDTH:]
        q2 = q * q
        k2 = k * k
        tail = x[:, 4 * WIDTH:]
        beta_all = _sigmoid(tail)
        g_all = -jnp.exp(alog_ref[...]) * _softplus(tail + dtb_ref[...])
        gated = nw_ref[...] * _silu(x[:, 3 * WIDTH:4 * WIDTH]) if readout else None
        yield
        yield
        q_ssq = _head_sum(q2, ones_ref)
        k_ssq = _head_sum(k2, ones_ref)
        gcs = _dot_f32(cum, g_all)
        yield
        yield
        q = q * (lax.rsqrt(q_ssq + EPS) * HEAD ** -0.5)
        k = k * lax.rsqrt(k_ssq + EPS)
        gcs_t = gcs.T
        gc = _split_cols(gcs, first_g)
        gr = jnp.stack([gcs_t[first_g + h:first_g + h + 1, ci * CHUNK:(ci + 1) * CHUNK]
                        for ci in range(nc) for h in range(HEADS)], axis=0)
        beta = _split_cols(beta_all, d * HEADS)
        yield
        decay = jnp.where(incl, jnp.exp(jnp.minimum(gc - gr, 0.0)), 0.0)
        yield
        qs = _split_heads(q)
        ks = _split_heads(k)
        yield
        vs = _split_heads(v)
        kb = ks * beta
        eg = jnp.exp(gc)
        g_last = gc[:, edge:edge + 1]
        yield
        return dict(lhs=jnp.concatenate([kb, qs], axis=1), ks=ks, decay=decay,
                    vw=jnp.concatenate([vs * beta, kb * eg], axis=2), q_eg=qs * eg,
                    kd=ks * jnp.exp(g_last - gc), dec_last=jnp.exp(g_last), gated=gated)

    def chain(si, op, s):
        rows = slice(si * sub, (si + 1) * sub)
        big = _bmm(op["lhs"], op["ks"], tb=True)
        yield
        m = jnp.where(strict, big[:, :CHUNK] * op["decay"], 0.0)
        attn = big[:, CHUNK:] * op["decay"]
        yield
        tinv = yield from _inv_unit_tri(m)
        uw = _bmm(tinv, op["vw"])
        yield
        au = _bmm(attn, uw)
        yield
        q_eff = op["q_eg"] - au[:, :, HEAD:]
        y_in = au[:, :, :HEAD]
        kdu = _bmm(op["kd"], uw, ta=True)
        yield
        first = si * nc
        for ci in _block_order(nc, rev):
            sl = slice(ci * HEADS, (ci + 1) * HEADS)
            _store_heads(obuf, first + ci, _bmm(q_eff[sl], s) + y_in[sl])
            s = s * op["dec_last"][sl] - _bmm(kdu[sl, :, HEAD:], s) + kdu[sl, :, :HEAD]
            yield
        o = obuf[rows]
        if readout:
            o = o + of_ref[0, rows]
            scale = lax.rsqrt(_head_sum(o * o, ones_ref) * (1.0 / HEAD) + EPS)
            o = o * scale * op["gated"]
        y_ref[0, rows] = o
        return s

    order = list(_block_order(bt // sub, rev))
    s = state[...]
    op = _run(prepare(order[0]))
    for n, si in enumerate(order):
        if n + 1 < len(order):
            s, op = _interleave(chain(si, op, s), prepare(order[n + 1]), CHAIN_LEAD)
        else:
            s = _run(chain(si, op, s))
    state[...] = s

    @pl.when(j == n_blocks - 1)
    def _emit_state():
        so_ref[0] = state[...]


def _block_rows(l, chunks):
    return min(chunks * CHUNK, l)


def _halo_specs(width, bt, n_blocks, rev, halo):
    per_block = bt // halo
    last = n_blocks * per_block - 1

    def cur(i, j):
        return (i, _chunk_index(j, n_blocks, rev), 0)

    def prev(i, j):
        return (i, jnp.maximum(_chunk_index(j, n_blocks, rev) * per_block - 1, 0), 0)

    def nxt(i, j):
        return (i, jnp.minimum((_chunk_index(j, n_blocks, rev) + 1) * per_block, last), 0)

    return [pl.BlockSpec((1, halo, width), prev), pl.BlockSpec((1, bt, width), cur),
            pl.BlockSpec((1, halo, width), nxt)]


def _gdn_call(p, prm, d, s0, rev, readout, of=None):
    b, l, _ = p.shape
    bt = _block_rows(l, BLOCK_CHUNKS)
    n_blocks = l // bt
    tok = pl.BlockSpec((1, bt, WIDTH), lambda i, j: (i, _chunk_index(j, n_blocks, rev), 0))
    st = pl.BlockSpec((1, HEADS, HEAD, HEAD), lambda i, j: (i, 0, 0, 0))

    def full(arr):
        return pl.BlockSpec(arr.shape, lambda i, j: (0,) * arr.ndim)

    args = [p, p, p, prm["conv"], prm["alog"][d], prm["dtb"][d], prm["ones"], s0]
    specs = _halo_specs(GDN_COLS_PACKED, bt, n_blocks, rev, SUBLANES) + [
        full(prm["conv"]), full(prm["alog"][d]), full(prm["dtb"][d]), full(prm["ones"]), st]
    if readout:
        args += [prm["norm"], of]
        specs += [full(prm["norm"]), tok]
    return pl.pallas_call(
        functools.partial(_gdn_kernel, rev=rev, readout=readout, n_blocks=n_blocks, d=d),
        grid=(b, n_blocks),
        in_specs=specs,
        out_specs=[tok, st],
        out_shape=[jax.ShapeDtypeStruct((b, l, WIDTH), F32),
                   jax.ShapeDtypeStruct((b, HEADS, HEAD, HEAD), F32)],
        scratch_shapes=[pltpu.VMEM((HEADS, HEAD, HEAD), F32), pltpu.VMEM((bt, WIDTH), F32)],
        compiler_params=_cparams(("parallel", "arbitrary")),
        name="gdn_bwd" if rev else "gdn_fwd",
    )(*args)


def _gdn_mixer(p_ctx, p_lat, prm):
    b = p_lat.shape[0]
    zero = jnp.zeros((b, HEADS, HEAD, HEAD), F32)
    oc_f, sc_f = _gdn_call(p_ctx, prm, 0, zero, False, False)
    y_ctx, sc_b = _gdn_call(p_ctx, prm, 1, zero, True, True, oc_f)
    ol_f, _ = _gdn_call(p_lat, prm, 0, sc_f, False, False)
    y_lat, _ = _gdn_call(p_lat, prm, 1, sc_b, True, True, ol_f)
    return y_ctx, y_lat


def _rwkv_kernel(*refs, rev, readout, grid_shift, n_blocks):
    (pp_ref, pc_ref, pn_ref, mu_ref, w0_ref, wup_ref, a0_ref, aup_ref, kk_ref, ka_ref, rk_ref,
     ones_ref, s0_ref) = refs[:13]
    if readout:
        gup_ref, lnw_ref, lnb_ref, yf_ref, bf_ref = refs[13:18]
        y_ref, so_ref = refs[18:20]
        state, obuf = refs[20:]
    else:
        y_ref, bon_ref, so_ref = refs[13:16]
        state, obuf = refs[16:]
    bt = pc_ref.shape[1]
    nc = bt // CHUNK
    j = pl.program_id(1)
    c = _chunk_index(j, n_blocks, rev)

    @pl.when(j == 0)
    def _load_state():
        state[...] = s0_ref[0]

    halo = pp_ref.shape[1]
    ext = jnp.concatenate([_halo_rows(pp_ref, c > 0), pc_ref[0],
                           _halo_rows(pn_ref, c < n_blocks - 1)], axis=0)
    incl = _tri_mask(rev, False)
    strict = _tri_mask(rev, True)
    sub = min(SUB_CHUNKS * CHUNK, bt)
    cum = _chunk_cumsum_matrix(sub, rev)

    def prepare(si):
        lo = si * sub
        seg = ext[lo:lo + sub + 2 * halo]
        x = seg[halo:halo + sub]
        lane = lax.broadcasted_iota(jnp.int32, x.shape, 1)
        if grid_shift:
            col = lax.broadcasted_iota(jnp.int32, x.shape, 0) & (GRID_W - 1)
            left = jnp.where(col == 0, 0.0, pltpu.roll(x, 1, 0))
            right = jnp.where(col == GRID_W - 1, 0.0, pltpu.roll(x, sub - 1, 0))
            cm = lane & 3
            shifted = jnp.where(cm == 0, left, jnp.where(
                cm == 1, right, jnp.where(cm == 2, seg[:sub], seg[2 * halo:])))
        else:
            shifted = jnp.where((lane & 1) == 0, pltpu.roll(seg, 1, 0)[halo:halo + sub],
                                pltpu.roll(seg, sub + 2 * halo - 1, 0)[halo:halo + sub])
        yield
        pm = x + (shifted - x) * mu_ref[...]
        r = pm[:, :WIDTH]
        k = pm[:, WIDTH:2 * WIDTH]
        v = pm[:, 2 * WIDTH:3 * WIDTH]
        o1 = 3 * WIDTH
        w_dn = pm[:, o1:o1 + 2 * LORA]
        a_dn = pm[:, o1 + 2 * LORA:o1 + 4 * LORA]
        g_dn = pm[:, o1 + 4 * LORA:]
        w_act = jnp.tanh(w_dn)
        g_act = _sigmoid(g_dn)
        kkr = k * kk_ref[...]
        kkr2 = kkr * kkr
        yield
        yield
        w_lora = _dot(w_act, wup_ref[...])
        a_lora = _dot(a_dn, aup_ref[...])
        kk_ssq = _head_sum(kkr2, ones_ref)
        gate = _dot(g_act, gup_ref[...]) if readout else None
        yield
        yield
        w_log = -_softplus(-(w0_ref[...] + w_lora)) - 0.5
        lw = -jnp.exp(w_log)
        iclr = _sigmoid(a0_ref[...] + a_lora)
        kk = kkr * lax.rsqrt(kk_ssq + EPS)
        k_dir = k * (1.0 + (iclr - 1.0) * ka_ref[...])
        b_dir = kk * iclr
        rkr = r * k_dir * rk_ref[...]
        yield
        yield
        g_inc = _dot_f32(cum, lw)
        bonus = _head_sum(rkr, ones_ref) * v
        yield
        yield
        g_end = _chunk_edge_rows(g_inc, rev)
        e_neg = jnp.exp(-g_inc)
        e_end = jnp.exp(g_end - g_inc)
        yield
        lhs = jnp.concatenate([_split_heads(kk * jnp.exp(g_inc - lw)),
                               _split_heads(r * jnp.exp(g_inc))], axis=1)
        yield
        rhs = jnp.concatenate([_split_heads(b_dir * e_neg), _split_heads(k_dir * e_neg)], axis=1)
        yield
        b_eh = _split_heads(b_dir * e_end)
        yield
        k_eh = _split_heads(k_dir * e_end)
        yield
        v_h = _split_heads(v)
        dec_end = _split_heads(jnp.exp(g_end))[:, :1, :]
        return dict(lhs=lhs, rhs=rhs, b_eh=b_eh, k_eh=k_eh, v_h=v_h, dec_end=dec_end,
                    bonus=bonus, gate=gate)

    def chain(si, op, s):
        rows = slice(si * sub, (si + 1) * sub)
        kap_h = op["lhs"][:, :CHUNK]
        r_h = op["lhs"][:, CHUNK:]
        v_h = op["v_h"]
        big = _bmm(op["lhs"], op["rhs"], tb=True)
        yield
        a_b = jnp.where(strict, big[:, :CHUNK, :CHUNK], 0.0)
        a_k = jnp.where(strict, big[:, :CHUNK, CHUNK:], 0.0)
        r_b = jnp.where(incl, big[:, CHUNK:, :CHUNK], 0.0)
        r_k = jnp.where(incl, big[:, CHUNK:, CHUNK:], 0.0)
        akv = _bmm(a_k, v_h)
        yield
        tinv = yield from _inv_unit_tri(a_b)
        z = _bmm(tinv, jnp.concatenate([kap_h, akv], axis=2))
        yield
        rz = _bmm(r_b, z)
        yield
        q_eff = r_h - rz[:, :, :HEAD]
        y_in = _bmm(r_k, v_h) - rz[:, :, HEAD:]
        yield
        ztb = _bmm(z, op["b_eh"], ta=True)
        yield
        pt = ztb[:, :HEAD]
        xt = _bmm(v_h, op["k_eh"], ta=True) - ztb[:, HEAD:]
        yield
        first = si * (sub // CHUNK)
        for ci in _block_order(sub // CHUNK, rev):
            sl = slice(ci * HEADS, (ci + 1) * HEADS)
            _store_heads(obuf, first + ci, _bmm(q_eff[sl], s, tb=True) + y_in[sl])
            s = s * op["dec_end"][sl] - _bmm(s, pt[sl]) + xt[sl]
            yield
        y = obuf[rows]
        if readout:
            y = y + yf_ref[0, rows]
            yc = y - _head_sum(y, ones_ref) * (1.0 / HEAD)
            var = _head_sum(yc * yc, ones_ref) * (1.0 / HEAD)
            yn = yc * lax.rsqrt(var + RWKV_GN_EPS) * lnw_ref[...] + lnb_ref[...]
            y_ref[0, rows] = (yn + op["bonus"] + bf_ref[0, rows]) * op["gate"]
        else:
            y_ref[0, rows] = y
            bon_ref[0, rows] = op["bonus"]
        return s

    order = list(_block_order(bt // sub, rev))
    s = state[...]
    op = _run(prepare(order[0]))
    for n, si in enumerate(order):
        if n + 1 < len(order):
            s, op = _interleave(chain(si, op, s), prepare(order[n + 1]), CHAIN_LEAD)
        else:
            s = _run(chain(si, op, s))
    state[...] = s

    @pl.when(j == n_blocks - 1)
    def _emit_state():
        so_ref[0] = state[...]


def _rwkv_call(p, prm, d, s0, rev, readout, grid_shift, yf=None, bf=None):
    b, l, _ = p.shape
    bt = _block_rows(l, BLOCK_CHUNKS)
    n_blocks = l // bt
    tok = pl.BlockSpec((1, bt, WIDTH), lambda i, j: (i, _chunk_index(j, n_blocks, rev), 0))
    st = pl.BlockSpec((1, HEADS, HEAD, HEAD), lambda i, j: (i, 0, 0, 0))

    def full(arr):
        return pl.BlockSpec(arr.shape, lambda i, j: (0,) * arr.ndim)

    small = [prm["mu"], prm["w0"][d], prm["w_up"][d], prm["a0"][d], prm["a_up"][d],
             prm["k_k"], prm["k_a"], prm["r_k"], prm["ones"]]
    args = [p, p, p] + small + [s0]
    halo = GRID_W if grid_shift else SUBLANES
    specs = _halo_specs(RWKV_COLS, bt, n_blocks, rev, halo) + [full(a) for a in small] + [st]
    tok_shape = jax.ShapeDtypeStruct((b, l, WIDTH), F32)
    st_shape = jax.ShapeDtypeStruct((b, HEADS, HEAD, HEAD), F32)
    if readout:
        extra = [prm["g_up"], prm["ln_w"], prm["ln_b"]]
        args += extra + [yf, bf]
        specs += [full(a) for a in extra] + [tok, tok]
        out_specs, out_shape = [tok, st], [tok_shape, st_shape]
    else:
        out_specs, out_shape = [tok, tok, st], [tok_shape, tok_shape, st_shape]
    return pl.pallas_call(
        functools.partial(_rwkv_kernel, rev=rev, readout=readout, grid_shift=grid_shift,
                          n_blocks=n_blocks),
        grid=(b, n_blocks),
        in_specs=specs,
        out_specs=out_specs,
        out_shape=out_shape,
        scratch_shapes=[pltpu.VMEM((HEADS, HEAD, HEAD), F32), pltpu.VMEM((bt, WIDTH), F32)],
        compiler_params=_cparams(("parallel", "arbitrary")),
        name="rwkv_bwd" if rev else "rwkv_fwd",
    )(*args)


def _rwkv_mixer(p_ctx, p_lat, prm):
    b = p_lat.shape[0]
    zero = jnp.zeros((b, HEADS, HEAD, HEAD), F32)
    yc_f, bc_f, sc_f = _rwkv_call(p_ctx, prm, 0, zero, False, False, False)
    y_ctx, sc_b = _rwkv_call(p_ctx, prm, 1, zero, True, True, False, yc_f, bc_f)
    yl_f, bl_f, _ = _rwkv_call(p_lat, prm, 0, sc_f, False, False, True)
    y_lat, _ = _rwkv_call(p_lat, prm, 1, sc_b, True, True, True, yl_f, bl_f)
    return y_ctx, y_lat


def _pack_w_in(w):
    o_gdn = S5_WIDTH
    o_tail = o_gdn + 4 * WIDTH
    o_rwkv = o_tail + 4 * HEADS
    pad = jnp.zeros((w.shape[0], GDN_TAIL - 4 * HEADS), w.dtype)
    return jnp.concatenate([w[:, :o_tail], w[:, o_tail:o_rwkv], pad, w[:, o_rwkv:]], axis=1)


def _block_diag(blocks):
    g, r, c = blocks.shape
    eye = jnp.eye(g, dtype=blocks.dtype)
    return (eye[:, None, :, None] * blocks[:, :, None, :]).reshape(g * r, g * c)


def _s5_params(layer, b_re, b_im, c_re, c_im, d, a_re, a_im, log_dt, glu_w, glu_b):
    row = lambda v: v.reshape(2, 1, S5_N)
    return {
        "a_re": row(a_re[layer]), "a_im": row(a_im[layer]),
        "log_dt": row(jnp.repeat(log_dt[layer], S5_STATE, axis=-1)),
        "b_re": _block_diag(jnp.swapaxes(b_re[layer], 1, 2)),
        "b_im": _block_diag(jnp.swapaxes(b_im[layer], 1, 2)),
        "c_re": _block_diag(jnp.swapaxes(c_re[layer], 1, 2)),
        "c_im": _block_diag(jnp.swapaxes(c_im[layer], 1, 2)),
        "d": d[layer].reshape(1, S5_WIDTH), "glu_w": glu_w[layer],
        "glu_b": glu_b[layer].reshape(1, S5_WIDTH),
    }


def _head_ones():
    return _block_diag(jnp.ones((HEADS, HEAD, HEAD), BF16))


def _gdn_params(layer, conv, a_log, dt_bias, norm):
    def tail_row(v):
        rows = []
        for d in range(2):
            z = jnp.zeros((GDN_TAIL,), F32)
            rows.append(z.at[2 * HEADS + d * HEADS:2 * HEADS + (d + 1) * HEADS].set(v[d])
                        .reshape(1, GDN_TAIL))
        return rows

    return {"conv": conv[layer], "alog": tail_row(a_log[layer]), "dtb": tail_row(dt_bias[layer]),
            "norm": jnp.tile(norm[layer], HEADS).reshape(1, WIDTH), "ones": _head_ones()}


def _rwkv_params(layer, mu, w0, w_up, a0, a_up, g_up, k_k, k_a, r_k, ln_w, ln_b):
    def lora_pad(w):
        z = jnp.zeros((LORA, WIDTH), F32)
        return [jnp.concatenate([w[0], z], axis=0), jnp.concatenate([z, w[1]], axis=0)]

    row = lambda v: v.reshape(1, -1)
    return {
        "mu": row(mu[layer]), "w0": [row(w0[layer, 0]), row(w0[layer, 1])],
        "w_up": lora_pad(w_up[layer]), "a0": [row(a0[layer, 0]), row(a0[layer, 1])],
        "a_up": lora_pad(a_up[layer]), "g_up": g_up[layer], "k_k": row(k_k[layer]),
        "k_a": row(k_a[layer]), "r_k": row(r_k[layer]), "ln_w": row(ln_w[layer]),
        "ln_b": row(ln_b[layer]), "ones": _head_ones(),
    }


def kernel(x, c, ctx, c_ctx, mod_w, mod_b, norm_mix, norm_mlp, norm_final, w_in, w_out, s5_b_re, s5_b_im, s5_c_re, s5_c_im, s5_d, s5_a_re, s5_a_im, s5_log_dt, s5_glu_w, s5_glu_b, gdn_conv, gdn_a_log, gdn_dt_bias, gdn_norm, rwkv_mu, rwkv_w0, rwkv_w_up, rwkv_a0, rwkv_a_up, rwkv_g_up, rwkv_k_k, rwkv_k_a, rwkv_r_k, rwkv_ln_w, rwkv_ln_b, mlp_w1, mlp_w2):
    depth = mod_w.shape[0]
    batch = x.shape[0]
    x = x.astype(F32)
    ctx = ctx.astype(F32)
    rows = SUBLANES * ((batch + 1 + SUBLANES - 1) // SUBLANES)
    cvec = jnp.zeros((rows, D_MODEL), F32).at[:batch].set(c.astype(F32)).at[batch].set(
        c_ctx.astype(F32))
    mods_all = _mods(cvec, mod_w, mod_b)

    for layer in range(depth):
        ctx_out = layer < depth - 1
        m = mods_all[layer].reshape(rows, N_MOD, 1, D_MODEL)
        lat = [m[:batch, i] for i in range(N_MOD)]
        cm = [jnp.broadcast_to(m[batch, i][None], (batch, 1, D_MODEL)) for i in range(N_MOD)]
        w_packed = _pack_w_in(w_in[layer]).astype(BF16)
        wo = w_out[layer].astype(BF16)
        wo_parts = (wo[:S5_WIDTH], wo[S5_WIDTH:S5_WIDTH + WIDTH], wo[S5_WIDTH + WIDTH:])
        w1 = mlp_w1[layer].astype(BF16)
        w2 = mlp_w2[layer].astype(BF16)

        pa_l, pb_l, pc_l = _inproj(x, norm_mix[layer], lat[1], lat[0], w_packed)
        pa_c, pb_c, pc_c = _inproj(ctx, norm_mix[layer], cm[1], cm[0], w_packed)

        ya_c, ya_l = _s5_mixer(pa_c, pa_l, _s5_params(
            layer, s5_b_re, s5_b_im, s5_c_re, s5_c_im, s5_d, s5_a_re, s5_a_im, s5_log_dt,
            s5_glu_w, s5_glu_b))
        yb_c, yb_l = _gdn_mixer(pb_c, pb_l, _gdn_params(
            layer, gdn_conv, gdn_a_log, gdn_dt_bias, gdn_norm))
        yc_c, yc_l = _rwkv_mixer(pc_c, pc_l, _rwkv_params(
            layer, rwkv_mu, rwkv_w0, rwkv_w_up, rwkv_a0, rwkv_a_up, rwkv_g_up, rwkv_k_k,
            rwkv_k_a, rwkv_r_k, rwkv_ln_w, rwkv_ln_b))

        x = _outproj(x, ya_l, yb_l, yc_l, wo_parts, lat[2], norm_mlp[layer], lat[4], lat[3],
                     lat[5], w1, w2, norm_final, final=not ctx_out)
        if ctx_out:
            ctx = _outproj(ctx, ya_c, yb_c, yc_c, wo_parts, cm[2], norm_mlp[layer], cm[4], cm[3],
                           cm[5], w1, w2, norm_final, final=False)
    return x
```

```python
import functools
import math

import jax
import jax.numpy as jnp
from jax import lax
from jax.experimental import pallas as pl
from jax.experimental.pallas import tpu as pltpu

F32 = jnp.float32
BF16 = jnp.bfloat16
HIGHEST = lax.Precision.HIGHEST

D_MODEL = 1024
D_FF = 4 * D_MODEL
N_MOD = 6
EPS = 1e-6
GRID_W = 64

S5_WIDTH = 256
S5_GROUP = 16
S5_GROUPS = 16
S5_STATE = 64
S5_N = S5_GROUPS * S5_STATE
S5_BLOCK = 512

HEAD = 64
HEADS = 6
WIDTH = HEADS * HEAD
GDN_CONV = 5
GDN_TAIL = 128
GDN_COLS_PACKED = 4 * WIDTH + GDN_TAIL
LORA = 64
GATE_LORA = 128
RWKV_COLS = 3 * WIDTH + 2 * LORA + 2 * LORA + GATE_LORA
RWKV_GN_EPS = 64e-5
CHUNK = 64
SUBLANES = 8
BLOCK_CHUNKS = 16
SUB_CHUNKS = 4
CHAIN_LEAD = 2

ROW_TILE = 512


def _dot(a, b, dims=((1,), (0,))):
    return lax.dot_general(a.astype(BF16), b.astype(BF16), (dims, ((), ())),
                           preferred_element_type=F32)


def _dot_f32(a, b, dims=((1,), (0,))):
    return lax.dot_general(a, b, (dims, ((), ())), precision=HIGHEST,
                           preferred_element_type=F32)


_NT = ((1,), (1,))
_TN = ((0,), (0,))


def _sigmoid(x):
    return 0.5 + 0.5 * jnp.tanh(0.5 * x)


def _silu(x):
    h = 0.5 * x
    return h + h * jnp.tanh(h)


def _softplus(x):
    return jnp.maximum(x, 0.0) + jnp.log1p(jnp.exp(-jnp.abs(x)))


def _bmm(a, b, ta=False, tb=False):
    dims = (((1 if ta else 2,), (2 if tb else 1,)), ((0,), (0,)))
    return lax.dot_general(a.astype(BF16), b.astype(BF16), dims, preferred_element_type=F32)


def _tri_mask(rev, strict):
    shape = (1, CHUNK, CHUNK)
    r = lax.broadcasted_iota(jnp.int32, shape, 1)
    c = lax.broadcasted_iota(jnp.int32, shape, 2)
    if rev:
        return (c > r) if strict else (c >= r)
    return (c < r) if strict else (c <= r)


def _chunk_cumsum_matrix(n, rev):
    r = lax.broadcasted_iota(jnp.int32, (n, n), 0)
    c = lax.broadcasted_iota(jnp.int32, (n, n), 1)
    shift = int(math.log2(CHUNK))
    same = (r >> shift) == (c >> shift)
    return jnp.where(same & ((c >= r) if rev else (c <= r)), 1.0, 0.0).astype(F32)


def _inv_unit_tri(a):
    n = a.shape[-1]
    r = lax.broadcasted_iota(jnp.int32, (1, n, n), 1)
    c = lax.broadcasted_iota(jnp.int32, (1, n, n), 2)
    inv = jnp.where(r == c, 1.0, 0.0).astype(F32) - jnp.where((r >> 1) == (c >> 1), a, 0.0)
    for lvl in range(1, int(math.log2(n))):
        couple = ((r >> (lvl + 1)) == (c >> (lvl + 1))) & ((r >> lvl) != (c >> lvl))
        half = _bmm(inv, jnp.where(couple, a, 0.0))
        yield
        inv = inv - _bmm(half, inv)
        yield
    return inv


def _run(gen):
    while True:
        try:
            next(gen)
        except StopIteration as stop:
            return stop.value


def _interleave(main, side, lead):
    results = {}
    live = {"main": main, "side": side}
    turn = 0
    while live:
        for name in ("main", "side"):
            if name in live and not (name == "side" and turn < lead and "main" in live):
                try:
                    next(live[name])
                except StopIteration as stop:
                    results[name] = stop.value
                    del live[name]
        turn += 1
    return results["main"], results["side"]


def _split_heads(x):
    nc = x.shape[0] // CHUNK
    return jnp.stack([x[ci * CHUNK:(ci + 1) * CHUNK, h * HEAD:(h + 1) * HEAD]
                      for ci in range(nc) for h in range(HEADS)], axis=0)


def _split_cols(x, first):
    nc = x.shape[0] // CHUNK
    return jnp.stack([x[ci * CHUNK:(ci + 1) * CHUNK, first + h:first + h + 1]
                      for ci in range(nc) for h in range(HEADS)], axis=0)


def _chunk_edge_rows(x, rev):
    nc = x.shape[0] // CHUNK
    edge = 0 if rev else CHUNK - 1
    return jnp.concatenate(
        [jnp.broadcast_to(x[ci * CHUNK + edge:ci * CHUNK + edge + 1], (CHUNK, x.shape[1]))
         for ci in range(nc)], axis=0)


def _head_sum(x, ones_ref):
    return _dot(x, ones_ref[...])


def _halo_rows(ref, keep):
    return jnp.where(keep, ref[0], 0.0)


def _chunk_index(j, n_chunks, rev):
    return (n_chunks - 1 - j) if rev else j


def _cparams(sem):
    return pltpu.CompilerParams(dimension_semantics=sem)


def _mods_kernel(c_ref, w_ref, b_ref, o_ref):
    o_ref[0] = _dot_f32(_silu(c_ref[...]), w_ref[0]) + b_ref[0]


def _mods(cvec, mod_w, mod_b):
    depth = mod_w.shape[0]
    rows = cvec.shape[0]
    n_tile = 1536
    return pl.pallas_call(
        _mods_kernel,
        grid=(depth, N_MOD * D_MODEL // n_tile),
        in_specs=[
            pl.BlockSpec((rows, D_MODEL), lambda l, n: (0, 0)),
            pl.BlockSpec((1, D_MODEL, n_tile), lambda l, n: (l, 0, n)),
            pl.BlockSpec((1, 1, n_tile), lambda l, n: (l, 0, n)),
        ],
        out_specs=pl.BlockSpec((1, rows, n_tile), lambda l, n: (l, 0, n)),
        out_shape=jax.ShapeDtypeStruct((depth, rows, N_MOD * D_MODEL), F32),
        compiler_params=_cparams(("arbitrary", "arbitrary")),
        name="mods",
    )(cvec, mod_w, mod_b.reshape(depth, 1, N_MOD * D_MODEL))


def _inproj_kernel(x_ref, g_ref, sc_ref, sh_ref, w_ref, oa_ref, ob_ref, oc_ref):
    x = x_ref[0]
    xn = x * lax.rsqrt(jnp.mean(x * x, axis=-1, keepdims=True) + EPS) * g_ref[...]
    xm = xn * (1.0 + sc_ref[0]) + sh_ref[0]
    p = _dot(xm, w_ref[0])
    oa_ref[0] = p[:, :S5_WIDTH]
    ob_ref[0] = p[:, S5_WIDTH:S5_WIDTH + GDN_COLS_PACKED]
    oc_ref[0] = p[:, S5_WIDTH + GDN_COLS_PACKED:]


def _inproj(x, gain, sc, sh, w_packed, layer):
    b, l, _ = x.shape
    tm = min(ROW_TILE, l)
    cols = w_packed.shape[2]
    vec = pl.BlockSpec((1, 1, D_MODEL), lambda i, t: (i, 0, 0))
    return pl.pallas_call(
        _inproj_kernel,
        grid=(b, l // tm),
        in_specs=[
            pl.BlockSpec((1, tm, D_MODEL), lambda i, t: (i, t, 0)),
            pl.BlockSpec((1, D_MODEL), lambda i, t: (0, 0)),
            vec, vec,
            pl.BlockSpec((1, D_MODEL, cols), lambda i, t: (layer, 0, 0),
                         pipeline_mode=pl.Buffered(1)),
        ],
        out_specs=[
            pl.BlockSpec((1, tm, S5_WIDTH), lambda i, t: (i, t, 0)),
            pl.BlockSpec((1, tm, GDN_COLS_PACKED), lambda i, t: (i, t, 0)),
            pl.BlockSpec((1, tm, RWKV_COLS), lambda i, t: (i, t, 0)),
        ],
        out_shape=[
            jax.ShapeDtypeStruct((b, l, S5_WIDTH), F32),
            jax.ShapeDtypeStruct((b, l, GDN_COLS_PACKED), F32),
            jax.ShapeDtypeStruct((b, l, RWKV_COLS), F32),
        ],
        compiler_params=_cparams(("parallel", "arbitrary")),
        name="inproj",
    )(x, gain.reshape(1, D_MODEL), sc, sh, w_packed)


def _outproj_kernel(x_ref, ya_ref, yb_ref, yc_ref, wo_ref, gt1_ref,
                    g_ref, sc_ref, sh_ref, gt2_ref, w1_ref, w2_ref, gf_ref, o_ref, *, final):
    o_b = S5_WIDTH
    o_c = S5_WIDTH + WIDTH
    mix = (_dot(ya_ref[0], wo_ref[0, :o_b]) + _dot(yb_ref[0], wo_ref[0, o_b:o_c])
           + _dot(yc_ref[0], wo_ref[0, o_c:]))
    x1 = x_ref[0] + gt1_ref[0] * mix
    h = x1 * lax.rsqrt(jnp.mean(x1 * x1, axis=-1, keepdims=True) + EPS) * g_ref[...]
    h = h * (1.0 + sc_ref[0]) + sh_ref[0]
    a = jnp.maximum(_dot(h, w1_ref[0]), 0.0)
    x2 = x1 + gt2_ref[0] * _dot(a * a, w2_ref[0])
    if final:
        x2 = x2 * lax.rsqrt(jnp.mean(x2 * x2, axis=-1, keepdims=True) + EPS) * gf_ref[...]
    o_ref[0] = x2


def _outproj(x, ya, yb, yc, wo, gt1, gain, sc, sh, gt2, w1, w2, gain_final, final, layer):
    b, l, _ = x.shape
    tm = min(ROW_TILE, l)
    vec = pl.BlockSpec((1, 1, D_MODEL), lambda i, t: (i, 0, 0))
    row = pl.BlockSpec((1, D_MODEL), lambda i, t: (0, 0))

    def tok(width):
        return pl.BlockSpec((1, tm, width), lambda i, t: (i, t, 0))

    def stacked(arr):
        return pl.BlockSpec((1,) + arr.shape[1:], lambda i, t: (layer, 0, 0),
                            pipeline_mode=pl.Buffered(1))

    return pl.pallas_call(
        functools.partial(_outproj_kernel, final=final),
        grid=(b, l // tm),
        in_specs=[tok(D_MODEL), tok(S5_WIDTH), tok(WIDTH), tok(WIDTH),
                  stacked(wo), vec, row, vec, vec, vec, stacked(w1), stacked(w2), row],
        out_specs=tok(D_MODEL),
        out_shape=jax.ShapeDtypeStruct((b, l, D_MODEL), F32),
        compiler_params=_cparams(("parallel", "arbitrary")),
        name="outproj_mlp",
    )(x, ya, yb, yc, wo, gt1, gain.reshape(1, D_MODEL), sc, sh, gt2, w1, w2,
      gain_final.reshape(1, D_MODEL))


def _cmul(a, b):
    return a[0] * b[0] - a[1] * b[1], a[0] * b[1] + a[1] * b[0]


def _s5_kernel(*refs, rev, readout, n_chunks):
    (u_ref, are_ref, aim_ref, ldt_ref, bre_ref, bim_ref, cre_ref, cim_ref,
     h0re_ref, h0im_ref) = refs[:10]
    if readout:
        d_ref, gw_ref, gb_ref, yf_ref = refs[10:14]
        y_ref, hre_ref, him_ref = refs[14:17]
        wre, wim, abt, pw, carry, perm_ref = refs[17:]
    else:
        y_ref, hre_ref, him_ref = refs[10:13]
        wre, wim, abt, pw, carry, perm_ref = refs[13:]
    t = u_ref.shape[1]
    seg = t // SUBLANES
    j = pl.program_id(1)
    tile = (SUBLANES, S5_N)

    @pl.when(j == 0)
    def _seed_carry():
        carry[0:1, :] = h0re_ref[0]
        carry[1:2, :] = h0im_ref[0]

    @pl.when((pl.program_id(0) == 0) & (j == 0))
    def _discretise():
        dt = jnp.exp(ldt_ref[...])
        a_re = are_ref[...]
        a_im = aim_ref[...]
        mag = jnp.exp(a_re * dt)
        ab = (mag * jnp.cos(a_im * dt), mag * jnp.sin(a_im * dt))
        den = a_re * a_re + a_im * a_im
        xr = ab[0] - 1.0
        cr = (xr * a_re + ab[1] * a_im) / den
        ci = (ab[1] * a_re - xr * a_im) / den
        wre[...] = (bre_ref[...] * cr - bim_ref[...] * ci).astype(BF16)
        wim[...] = (bre_ref[...] * ci + bim_ref[...] * cr).astype(BF16)
        abt[0] = jnp.broadcast_to(ab[0], tile)
        abt[1] = jnp.broadcast_to(ab[1], tile)
        power = ab
        for k in range(seg):
            pw[0, k] = jnp.broadcast_to(power[0], tile)
            pw[1, k] = jnp.broadcast_to(power[1], tile)
            if k + 1 < seg:
                power = _cmul(power, ab)
        m_idx = lax.broadcasted_iota(jnp.int32, (t, t), 0)
        n_idx = lax.broadcasted_iota(jnp.int32, (t, t), 1)
        src = seg * (m_idx & (SUBLANES - 1)) + (m_idx >> int(math.log2(SUBLANES)))
        perm_ref[...] = jnp.where(n_idx == src, 1.0, 0.0).astype(BF16)

    u = u_ref[0]
    perm = perm_ref[...]
    u_p = _dot(perm, u)
    bur = _dot(u_p, wre[...])
    bui = _dot(u_p, wim[...])
    steps = range(seg - 1, -1, -1) if rev else range(seg)
    a_r = abt[0]
    a_i = abt[1]
    hr = jnp.zeros(tile, F32)
    hi = jnp.zeros(tile, F32)
    local = [None] * seg
    for k in steps:
        rows = slice(k * SUBLANES, (k + 1) * SUBLANES)
        hr, hi = a_r * hr - a_i * hi + bur[rows], a_r * hi + a_i * hr + bui[rows]
        local[k] = (hr, hi)
    a_seg = (pw[0, seg - 1][0:1], pw[1, seg - 1][0:1])
    c = (carry[0:1, :], carry[1:2, :])
    row = lax.broadcasted_iota(jnp.int32, tile, 0)
    c_re = jnp.zeros(tile, F32)
    c_im = jnp.zeros(tile, F32)
    for r in (range(SUBLANES - 1, -1, -1) if rev else range(SUBLANES)):
        c_re = jnp.where(row == r, c[0], c_re)
        c_im = jnp.where(row == r, c[1], c_im)
        c = (hr[r:r + 1] + a_seg[0] * c[0] - a_seg[1] * c[1],
             hi[r:r + 1] + a_seg[0] * c[1] + a_seg[1] * c[0])
    carry[0:1, :] = c[0]
    carry[1:2, :] = c[1]
    h_re = [None] * seg
    h_im = [None] * seg
    for n, k in enumerate(steps):
        h_re[k] = local[k][0] + pw[0, n] * c_re - pw[1, n] * c_im
        h_im[k] = local[k][1] + pw[0, n] * c_im + pw[1, n] * c_re
    y_p = (_dot(jnp.concatenate(h_re, axis=0), cre_ref[...])
           - _dot(jnp.concatenate(h_im, axis=0), cim_ref[...]))
    y_hi = y_p.astype(BF16)
    y = _dot(perm, y_hi, _TN) + _dot(perm, y_p - y_hi.astype(F32), _TN)
    c_re, c_im = c
    if readout:
        y = y + yf_ref[0] + d_ref[...] * u
        z = 0.5 * y * (1.0 + jnp.tanh(math.sqrt(2.0 / math.pi) * (y + 0.044715 * y * y * y)))
        y = z * _sigmoid(_dot(z, gw_ref[...]) + gb_ref[...])
    y_ref[0] = y

    @pl.when(j == n_chunks - 1)
    def _emit_state():
        hre_ref[0] = c_re
        him_ref[0] = c_im


def _s5_call(u, prm, d, h0, rev, readout, yf=None):
    b, l, _ = u.shape
    t = min(S5_BLOCK, l)
    n_chunks = l // t
    tok = pl.BlockSpec((1, t, S5_WIDTH), lambda i, j: (i, _chunk_index(j, n_chunks, rev), 0))
    row_n = pl.BlockSpec((1, S5_N), lambda i, j: (0, 0))
    st = pl.BlockSpec((1, 1, S5_N), lambda i, j: (i, 0, 0))

    def full(arr):
        return pl.BlockSpec(arr.shape, lambda i, j: (0,) * arr.ndim)

    args = [u, prm["a_re"][d], prm["a_im"][d], prm["log_dt"][d], prm["b_re"], prm["b_im"],
            prm["c_re"], prm["c_im"], h0[0], h0[1]]
    specs = [tok, row_n, row_n, row_n, full(prm["b_re"]), full(prm["b_im"]),
             full(prm["c_re"]), full(prm["c_im"]), st, st]
    if readout:
        args += [prm["d"], prm["glu_w"], prm["glu_b"], yf]
        specs += [full(prm["d"]), full(prm["glu_w"]), full(prm["glu_b"]), tok]
    y, hre, him = pl.pallas_call(
        functools.partial(_s5_kernel, rev=rev, readout=readout, n_chunks=n_chunks),
        grid=(b, n_chunks),
        in_specs=specs,
        out_specs=[tok, st, st],
        out_shape=[jax.ShapeDtypeStruct((b, l, S5_WIDTH), F32),
                   jax.ShapeDtypeStruct((b, 1, S5_N), F32),
                   jax.ShapeDtypeStruct((b, 1, S5_N), F32)],
        scratch_shapes=[pltpu.VMEM((S5_WIDTH, S5_N), BF16), pltpu.VMEM((S5_WIDTH, S5_N), BF16),
                        pltpu.VMEM((2, SUBLANES, S5_N), F32),
                        pltpu.VMEM((2, t // SUBLANES, SUBLANES, S5_N), F32),
                        pltpu.VMEM((2, S5_N), F32), pltpu.VMEM((t, t), BF16)],
        compiler_params=_cparams(("arbitrary", "arbitrary")),
        name="s5_bwd" if rev else "s5_fwd",
    )(*args)
    return y, (hre, him)


def _s5_mixer(u_ctx, u_lat, prm):
    b = u_lat.shape[0]
    zero = (jnp.zeros((b, 1, S5_N), F32), jnp.zeros((b, 1, S5_N), F32))
    yc_f, hc_f = _s5_call(u_ctx, prm, 0, zero, False, False)
    y_ctx, hc_b = _s5_call(u_ctx, prm, 1, zero, True, True, yc_f)
    yl_f, _ = _s5_call(u_lat, prm, 0, hc_f, False, False)
    y_lat, _ = _s5_call(u_lat, prm, 1, hc_b, True, True, yl_f)
    return y_ctx, y_lat


def _block_order(nc, rev):
    return range(nc - 1, -1, -1) if rev else range(nc)


def _store_heads(buf, ci, o):
    for h in range(HEADS):
        buf[ci * CHUNK:(ci + 1) * CHUNK, h * HEAD:(h + 1) * HEAD] = o[h]


def _gdn_kernel(*refs, rev, readout, n_blocks, d):
    pp_ref, pc_ref, pn_ref, cw_ref, alog_ref, dtb_ref, ones_ref, s0_ref = refs[:8]
    if readout:
        nw_ref, of_ref = refs[8:10]
        y_ref, so_ref = refs[10:12]
        state, obuf = refs[12:]
    else:
        y_ref, so_ref = refs[8:10]
        state, obuf = refs[10:]
    bt = pc_ref.shape[1]
    nc = bt // CHUNK
    j = pl.program_id(1)
    c = _chunk_index(j, n_blocks, rev)

    @pl.when(j == 0)
    def _load_state():
        state[...] = s0_ref[0]

    halo = pp_ref.shape[1]
    ext = jnp.concatenate([_halo_rows(pp_ref, c > 0), pc_ref[0],
                           _halo_rows(pn_ref, c < n_blocks - 1)], axis=0)
    incl = _tri_mask(rev, False)
    strict = _tri_mask(rev, True)
    sub = min(SUB_CHUNKS * CHUNK, bt)
    nc = sub // CHUNK
    cum = _chunk_cumsum_matrix(sub, rev)
    half = GDN_CONV // 2
    first_g = 2 * HEADS + d * HEADS
    edge = 0 if rev else CHUNK - 1

    def prepare(si):
        lo = si * sub
        seg = ext[lo:lo + sub + 2 * halo]
        x = seg[halo:halo + sub]
        qkv = seg[:, :3 * WIDTH]
        acc = cw_ref[half:half + 1, :] * x[:, :3 * WIDTH]
        for off in range(-half, half + 1):
            if off != 0:
                tap = pltpu.roll(qkv, (-off) % (sub + 2 * halo), 0)[halo:halo + sub]
                acc = acc + cw_ref[half + off:half + off + 1, :] * tap
                yield
        act = _silu(acc)
        q = act[:, :WIDTH]
        k = act[:, WIDTH:2 * WIDTH]
        v = act[:, 2 * WIDTH:]
        q2 = q * q
        k2 = k * k
        tail = x[:, 4 * WIDTH:]
        beta_all = _sigmoid(tail)
        g_all = -jnp.exp(alog_ref[...]) * _softplus(tail + dtb_ref[...])
        gated = nw_ref[...] * _silu(x[:, 3 * WIDTH:4 * WIDTH]) if readout else None
        yield
        yield
        q_ssq = _head_sum(q2, ones_ref)
        k_ssq = _head_sum(k2, ones_ref)
        gcs = _dot_f32(cum, g_all)
        yield
        yield
        q = q * (lax.rsqrt(q_ssq + EPS) * HEAD ** -0.5)
        k = k * lax.rsqrt(k_ssq + EPS)
        gcs_t = gcs.T
        gc = _split_cols(gcs, first_g)
        gr = jnp.stack([gcs_t[first_g + h:first_g + h + 1, ci * CHUNK:(ci + 1) * CHUNK]
                        for ci in range(nc) for h in range(HEADS)], axis=0)
        beta = _split_cols(beta_all, d * HEADS)
        yield
        decay = jnp.where(incl, jnp.exp(jnp.minimum(gc - gr, 0.0)), 0.0)
        yield
        qs = _split_heads(q)
        ks = _split_heads(k)
        yield
        vs = _split_heads(v)
        kb = ks * beta
        eg = jnp.exp(gc)
        g_last = gc[:, edge:edge + 1]
        yield
        return dict(lhs=jnp.concatenate([kb, qs], axis=1), ks=ks, decay=decay,
                    vw=jnp.concatenate([vs * beta, kb * eg], axis=2), q_eg=qs * eg,
                    kd=ks * jnp.exp(g_last - gc), dec_last=jnp.exp(g_last), gated=gated)

    def chain(si, op, s):
        rows = slice(si * sub, (si + 1) * sub)
        big = _bmm(op["lhs"], op["ks"], tb=True)
        yield
        m = jnp.where(strict, big[:, :CHUNK] * op["decay"], 0.0)
        attn = big[:, CHUNK:] * op["decay"]
        yield
        tinv = yield from _inv_unit_tri(m)
        uw = _bmm(tinv, op["vw"])
        yield
        au = _bmm(attn, uw)
        yield
        q_eff = op["q_eg"] - au[:, :, HEAD:]
        y_in = au[:, :, :HEAD]
        kdu = _bmm(op["kd"], uw, ta=True)
        yield
        first = si * nc
        for ci in _block_order(nc, rev):
            sl = slice(ci * HEADS, (ci + 1) * HEADS)
            _store_heads(obuf, first + ci, _bmm(q_eff[sl], s) + y_in[sl])
            s = s * op["dec_last"][sl] - _bmm(kdu[sl, :, HEAD:], s) + kdu[sl, :, :HEAD]
            yield
        o = obuf[rows]
        if readout:
            o = o + of_ref[0, rows]
            scale = lax.rsqrt(_head_sum(o * o, ones_ref) * (1.0 / HEAD) + EPS)
            o = o * scale * op["gated"]
        y_ref[0, rows] = o
        return s

    order = list(_block_order(bt // sub, rev))
    s = state[...]
    op = _run(prepare(order[0]))
    for n, si in enumerate(order):
        if n + 1 < len(order):
            s, op = _interleave(chain(si, op, s), prepare(order[n + 1]), CHAIN_LEAD)
        else:
            s = _run(chain(si, op, s))
    state[...] = s

    @pl.when(j == n_blocks - 1)
    def _emit_state():
        so_ref[0] = state[...]


def _block_rows(l, chunks):
    return min(chunks * CHUNK, l)


def _halo_specs(width, bt, n_blocks, rev, halo):
    per_block = bt // halo
    last = n_blocks * per_block - 1

    def cur(i, j):
        return (i, _chunk_index(j, n_blocks, rev), 0)

    def prev(i, j):
        return (i, jnp.maximum(_chunk_index(j, n_blocks, rev) * per_block - 1, 0), 0)

    def nxt(i, j):
        return (i, jnp.minimum((_chunk_index(j, n_blocks, rev) + 1) * per_block, last), 0)

    return [pl.BlockSpec((1, halo, width), prev), pl.BlockSpec((1, bt, width), cur),
            pl.BlockSpec((1, halo, width), nxt)]


def _gdn_call(p, prm, d, s0, rev, readout, of=None):
    b, l, _ = p.shape
    bt = _block_rows(l, BLOCK_CHUNKS)
    n_blocks = l // bt
    tok = pl.BlockSpec((1, bt, WIDTH), lambda i, j: (i, _chunk_index(j, n_blocks, rev), 0))
    st = pl.BlockSpec((1, HEADS, HEAD, HEAD), lambda i, j: (i, 0, 0, 0))

    def full(arr):
        return pl.BlockSpec(arr.shape, lambda i, j: (0,) * arr.ndim)

    args = [p, p, p, prm["conv"], prm["alog"][d], prm["dtb"][d], prm["ones"], s0]
    specs = _halo_specs(GDN_COLS_PACKED, bt, n_blocks, rev, SUBLANES) + [
        full(prm["conv"]), full(prm["alog"][d]), full(prm["dtb"][d]), full(prm["ones"]), st]
    if readout:
        args += [prm["norm"], of]
        specs += [full(prm["norm"]), tok]
    return pl.pallas_call(
        functools.partial(_gdn_kernel, rev=rev, readout=readout, n_blocks=n_blocks, d=d),
        grid=(b, n_blocks),
        in_specs=specs,
        out_specs=[tok, st],
        out_shape=[jax.ShapeDtypeStruct((b, l, WIDTH), F32),
                   jax.ShapeDtypeStruct((b, HEADS, HEAD, HEAD), F32)],
        scratch_shapes=[pltpu.VMEM((HEADS, HEAD, HEAD), F32), pltpu.VMEM((bt, WIDTH), F32)],
        compiler_params=_cparams(("parallel", "arbitrary")),
        name="gdn_bwd" if rev else "gdn_fwd",
    )(*args)


def _gdn_mixer(p_ctx, p_lat, prm):
    b = p_lat.shape[0]
    zero = jnp.zeros((b, HEADS, HEAD, HEAD), F32)
    oc_f, sc_f = _gdn_call(p_ctx, prm, 0, zero, False, False)
    y_ctx, sc_b = _gdn_call(p_ctx, prm, 1, zero, True, True, oc_f)
    ol_f, _ = _gdn_call(p_lat, prm, 0, sc_f, False, False)
    y_lat, _ = _gdn_call(p_lat, prm, 1, sc_b, True, True, ol_f)
    return y_ctx, y_lat


def _rwkv_kernel(*refs, rev, readout, grid_shift, n_blocks):
    (pp_ref, pc_ref, pn_ref, mu_ref, w0_ref, wup_ref, a0_ref, aup_ref, kk_ref, ka_ref, rk_ref,
     ones_ref, s0_ref) = refs[:13]
    if readout:
        gup_ref, lnw_ref, lnb_ref, yf_ref, bf_ref = refs[13:18]
        y_ref, so_ref = refs[18:20]
        state, obuf = refs[20:]
    else:
        y_ref, bon_ref, so_ref = refs[13:16]
        state, obuf = refs[16:]
    bt = pc_ref.shape[1]
    nc = bt // CHUNK
    j = pl.program_id(1)
    c = _chunk_index(j, n_blocks, rev)

    @pl.when(j == 0)
    def _load_state():
        state[...] = s0_ref[0]

    halo = pp_ref.shape[1]
    ext = jnp.concatenate([_halo_rows(pp_ref, c > 0), pc_ref[0],
                           _halo_rows(pn_ref, c < n_blocks - 1)], axis=0)
    incl = _tri_mask(rev, False)
    strict = _tri_mask(rev, True)
    sub = min(SUB_CHUNKS * CHUNK, bt)
    cum = _chunk_cumsum_matrix(sub, rev)

    def prepare(si):
        lo = si * sub
        seg = ext[lo:lo + sub + 2 * halo]
        x = seg[halo:halo + sub]
        lane = lax.broadcasted_iota(jnp.int32, x.shape, 1)
        if grid_shift:
            col = lax.broadcasted_iota(jnp.int32, x.shape, 0) & (GRID_W - 1)
            left = jnp.where(col == 0, 0.0, pltpu.roll(x, 1, 0))
            right = jnp.where(col == GRID_W - 1, 0.0, pltpu.roll(x, sub - 1, 0))
            cm = lane & 3
            shifted = jnp.where(cm == 0, left, jnp.where(
                cm == 1, right, jnp.where(cm == 2, seg[:sub], seg[2 * halo:])))
        else:
            shifted = jnp.where((lane & 1) == 0, pltpu.roll(seg, 1, 0)[halo:halo + sub],
                                pltpu.roll(seg, sub + 2 * halo - 1, 0)[halo:halo + sub])
        yield
        pm = x + (shifted - x) * mu_ref[...]
        r = pm[:, :WIDTH]
        k = pm[:, WIDTH:2 * WIDTH]
        v = pm[:, 2 * WIDTH:3 * WIDTH]
        o1 = 3 * WIDTH
        w_dn = pm[:, o1:o1 + 2 * LORA]
        a_dn = pm[:, o1 + 2 * LORA:o1 + 4 * LORA]
        g_dn = pm[:, o1 + 4 * LORA:]
        w_act = jnp.tanh(w_dn)
        g_act = _sigmoid(g_dn)
        kkr = k * kk_ref[...]
        kkr2 = kkr * kkr
        yield
        yield
        w_lora = _dot(w_act, wup_ref[...])
        a_lora = _dot(a_dn, aup_ref[...])
        kk_ssq = _head_sum(kkr2, ones_ref)
        gate = _dot(g_act, gup_ref[...]) if readout else None
        yield
        yield
        w_log = -_softplus(-(w0_ref[...] + w_lora)) - 0.5
        lw = -jnp.exp(w_log)
        iclr = _sigmoid(a0_ref[...] + a_lora)
        kk = kkr * lax.rsqrt(kk_ssq + EPS)
        k_dir = k * (1.0 + (iclr - 1.0) * ka_ref[...])
        b_dir = kk * iclr
        rkr = r * k_dir * rk_ref[...]
        yield
        yield
        g_inc = _dot_f32(cum, lw)
        bonus = _head_sum(rkr, ones_ref) * v
        yield
        yield
        g_end = _chunk_edge_rows(g_inc, rev)
        e_neg = jnp.exp(-g_inc)
        e_end = jnp.exp(g_end - g_inc)
        yield
        lhs = jnp.concatenate([_split_heads(kk * jnp.exp(g_inc - lw)),
                               _split_heads(r * jnp.exp(g_inc))], axis=1)
        yield
        rhs = jnp.concatenate([_split_heads(b_dir * e_neg), _split_heads(k_dir * e_neg)], axis=1)
        yield
        b_eh = _split_heads(b_dir * e_end)
        yield
        k_eh = _split_heads(k_dir * e_end)
        yield
        v_h = _split_heads(v)
        dec_end = _split_heads(jnp.exp(g_end))[:, :1, :]
        return dict(lhs=lhs, rhs=rhs, b_eh=b_eh, k_eh=k_eh, v_h=v_h, dec_end=dec_end,
                    bonus=bonus, gate=gate)

    def chain(si, op, s):
        rows = slice(si * sub, (si + 1) * sub)
        kap_h = op["lhs"][:, :CHUNK]
        r_h = op["lhs"][:, CHUNK:]
        v_h = op["v_h"]
        big = _bmm(op["lhs"], op["rhs"], tb=True)
        yield
        a_b = jnp.where(strict, big[:, :CHUNK, :CHUNK], 0.0)
        a_k = jnp.where(strict, big[:, :CHUNK, CHUNK:], 0.0)
        r_b = jnp.where(incl, big[:, CHUNK:, :CHUNK], 0.0)
        r_k = jnp.where(incl, big[:, CHUNK:, CHUNK:], 0.0)
        akv = _bmm(a_k, v_h)
        yield
        tinv = yield from _inv_unit_tri(a_b)
        z = _bmm(tinv, jnp.concatenate([kap_h, akv], axis=2))
        yield
        rz = _bmm(r_b, z)
        yield
        q_eff = r_h - rz[:, :, :HEAD]
        y_in = _bmm(r_k, v_h) - rz[:, :, HEAD:]
        yield
        ztb = _bmm(z, op["b_eh"], ta=True)
        yield
        pt = ztb[:, :HEAD]
        xt = _bmm(v_h, op["k_eh"], ta=True) - ztb[:, HEAD:]
        yield
        first = si * (sub // CHUNK)
        for ci in _block_order(sub // CHUNK, rev):
            sl = slice(ci * HEADS, (ci + 1) * HEADS)
            _store_heads(obuf, first + ci, _bmm(q_eff[sl], s, tb=True) + y_in[sl])
            s = s * op["dec_end"][sl] - _bmm(s, pt[sl]) + xt[sl]
            yield
        y = obuf[rows]
        if readout:
            y = y + yf_ref[0, rows]
            yc = y - _head_sum(y, ones_ref) * (1.0 / HEAD)
            var = _head_sum(yc * yc, ones_ref) * (1.0 / HEAD)
            yn = yc * lax.rsqrt(var + RWKV_GN_EPS) * lnw_ref[...] + lnb_ref[...]
            y_ref[0, rows] = (yn + op["bonus"] + bf_ref[0, rows]) * op["gate"]
        else:
            y_ref[0, rows] = y
            bon_ref[0, rows] = op["bonus"]
        return s

    order = list(_block_order(bt // sub, rev))
    s = state[...]
    op = _run(prepare(order[0]))
    for n, si in enumerate(order):
        if n + 1 < len(order):
            s, op = _interleave(chain(si, op, s), prepare(order[n + 1]), CHAIN_LEAD)
        else:
            s = _run(chain(si, op, s))
    state[...] = s

    @pl.when(j == n_blocks - 1)
    def _emit_state():
        so_ref[0] = state[...]


def _rwkv_call(p, prm, d, s0, rev, readout, grid_shift, yf=None, bf=None):
    b, l, _ = p.shape
    bt = _block_rows(l, BLOCK_CHUNKS)
    n_blocks = l // bt
    tok = pl.BlockSpec((1, bt, WIDTH), lambda i, j: (i, _chunk_index(j, n_blocks, rev), 0))
    st = pl.BlockSpec((1, HEADS, HEAD, HEAD), lambda i, j: (i, 0, 0, 0))

    def full(arr):
        return pl.BlockSpec(arr.shape, lambda i, j: (0,) * arr.ndim)

    small = [prm["mu"], prm["w0"][d], prm["w_up"][d], prm["a0"][d], prm["a_up"][d],
             prm["k_k"], prm["k_a"], prm["r_k"], prm["ones"]]
    args = [p, p, p] + small + [s0]
    halo = GRID_W if grid_shift else SUBLANES
    specs = _halo_specs(RWKV_COLS, bt, n_blocks, rev, halo) + [full(a) for a in small] + [st]
    tok_shape = jax.ShapeDtypeStruct((b, l, WIDTH), F32)
    st_shape = jax.ShapeDtypeStruct((b, HEADS, HEAD, HEAD), F32)
    if readout:
        extra = [prm["g_up"], prm["ln_w"], prm["ln_b"]]
        args += extra + [yf, bf]
        specs += [full(a) for a in extra] + [tok, tok]
        out_specs, out_shape = [tok, st], [tok_shape, st_shape]
    else:
        out_specs, out_shape = [tok, tok, st], [tok_shape, tok_shape, st_shape]
    return pl.pallas_call(
        functools.partial(_rwkv_kernel, rev=rev, readout=readout, grid_shift=grid_shift,
                          n_blocks=n_blocks),
        grid=(b, n_blocks),
        in_specs=specs,
        out_specs=out_specs,
        out_shape=out_shape,
        scratch_shapes=[pltpu.VMEM((HEADS, HEAD, HEAD), F32), pltpu.VMEM((bt, WIDTH), F32)],
        compiler_params=_cparams(("parallel", "arbitrary")),
        name="rwkv_bwd" if rev else "rwkv_fwd",
    )(*args)


def _rwkv_mixer(p_ctx, p_lat, prm):
    b = p_lat.shape[0]
    zero = jnp.zeros((b, HEADS, HEAD, HEAD), F32)
    yc_f, bc_f, sc_f = _rwkv_call(p_ctx, prm, 0, zero, False, False, False)
    y_ctx, sc_b = _rwkv_call(p_ctx, prm, 1, zero, True, True, False, yc_f, bc_f)
    yl_f, bl_f, _ = _rwkv_call(p_lat, prm, 0, sc_f, False, False, True)
    y_lat, _ = _rwkv_call(p_lat, prm, 1, sc_b, True, True, True, yl_f, bl_f)
    return y_ctx, y_lat


def _pack_w_in(w):
    o_rwkv = S5_WIDTH + 4 * WIDTH + 4 * HEADS
    pad = jnp.zeros(w.shape[:-1] + (GDN_TAIL - 4 * HEADS,), w.dtype)
    return jnp.concatenate([w[..., :o_rwkv], pad, w[..., o_rwkv:]], axis=-1)


def _block_diag(blocks):
    g, r, c = blocks.shape
    eye = jnp.eye(g, dtype=blocks.dtype)
    return (eye[:, None, :, None] * blocks[:, :, None, :]).reshape(g * r, g * c)


def _s5_params(layer, b_re, b_im, c_re, c_im, d, a_re, a_im, log_dt, glu_w, glu_b):
    row = lambda v: v.reshape(2, 1, S5_N)
    return {
        "a_re": row(a_re[layer]), "a_im": row(a_im[layer]),
        "log_dt": row(jnp.repeat(log_dt[layer], S5_STATE, axis=-1)),
        "b_re": _block_diag(jnp.swapaxes(b_re[layer], 1, 2)),
        "b_im": _block_diag(jnp.swapaxes(b_im[layer], 1, 2)),
        "c_re": _block_diag(jnp.swapaxes(c_re[layer], 1, 2)),
        "c_im": _block_diag(jnp.swapaxes(c_im[layer], 1, 2)),
        "d": d[layer].reshape(1, S5_WIDTH), "glu_w": glu_w[layer],
        "glu_b": glu_b[layer].reshape(1, S5_WIDTH),
    }


def _head_ones():
    return _block_diag(jnp.ones((HEADS, HEAD, HEAD), BF16))


def _gdn_params(layer, conv, a_log, dt_bias, norm):
    def tail_row(v):
        rows = []
        for d in range(2):
            z = jnp.zeros((GDN_TAIL,), F32)
            rows.append(z.at[2 * HEADS + d * HEADS:2 * HEADS + (d + 1) * HEADS].set(v[d])
                        .reshape(1, GDN_TAIL))
        return rows

    return {"conv": conv[layer], "alog": tail_row(a_log[layer]), "dtb": tail_row(dt_bias[layer]),
            "norm": jnp.tile(norm[layer], HEADS).reshape(1, WIDTH), "ones": _head_ones()}


def _rwkv_params(layer, mu, w0, w_up, a0, a_up, g_up, k_k, k_a, r_k, ln_w, ln_b):
    def lora_pad(w):
        z = jnp.zeros((LORA, WIDTH), F32)
        return [jnp.concatenate([w[0], z], axis=0), jnp.concatenate([z, w[1]], axis=0)]

    row = lambda v: v.reshape(1, -1)
    return {
        "mu": row(mu[layer]), "w0": [row(w0[layer, 0]), row(w0[layer, 1])],
        "w_up": lora_pad(w_up[layer]), "a0": [row(a0[layer, 0]), row(a0[layer, 1])],
        "a_up": lora_pad(a_up[layer]), "g_up": g_up[layer], "k_k": row(k_k[layer]),
        "k_a": row(k_a[layer]), "r_k": row(r_k[layer]), "ln_w": row(ln_w[layer]),
        "ln_b": row(ln_b[layer]), "ones": _head_ones(),
    }


def kernel(x, c, ctx, c_ctx, mod_w, mod_b, norm_mix, norm_mlp, norm_final, w_in, w_out, s5_b_re, s5_b_im, s5_c_re, s5_c_im, s5_d, s5_a_re, s5_a_im, s5_log_dt, s5_glu_w, s5_glu_b, gdn_conv, gdn_a_log, gdn_dt_bias, gdn_norm, rwkv_mu, rwkv_w0, rwkv_w_up, rwkv_a0, rwkv_a_up, rwkv_g_up, rwkv_k_k, rwkv_k_a, rwkv_r_k, rwkv_ln_w, rwkv_ln_b, mlp_w1, mlp_w2):
    depth = mod_w.shape[0]
    batch = x.shape[0]
    x = x.astype(F32)
    ctx = ctx.astype(F32)
    rows = SUBLANES * ((batch + 1 + SUBLANES - 1) // SUBLANES)
    cvec = jnp.zeros((rows, D_MODEL), F32).at[:batch].set(c.astype(F32)).at[batch].set(
        c_ctx.astype(F32))
    mods_all = _mods(cvec, mod_w, mod_b)
    w_packed = _pack_w_in(w_in.astype(BF16))
    wo = w_out.astype(BF16)
    w1 = mlp_w1.astype(BF16)
    w2 = mlp_w2.astype(BF16)

    for layer in range(depth):
        ctx_out = layer < depth - 1
        m = mods_all[layer].reshape(rows, N_MOD, 1, D_MODEL)
        lat = [m[:batch, i] for i in range(N_MOD)]
        cm = [jnp.broadcast_to(m[batch, i][None], (batch, 1, D_MODEL)) for i in range(N_MOD)]

        pa_l, pb_l, pc_l = _inproj(x, norm_mix[layer], lat[1], lat[0], w_packed, layer)
        pa_c, pb_c, pc_c = _inproj(ctx, norm_mix[layer], cm[1], cm[0], w_packed, layer)

        ya_c, ya_l = _s5_mixer(pa_c, pa_l, _s5_params(
            layer, s5_b_re, s5_b_im, s5_c_re, s5_c_im, s5_d, s5_a_re, s5_a_im, s5_log_dt,
            s5_glu_w, s5_glu_b))
        yb_c, yb_l = _gdn_mixer(pb_c, pb_l, _gdn_params(
            layer, gdn_conv, gdn_a_log, gdn_dt_bias, gdn_norm))
        yc_c, yc_l = _rwkv_mixer(pc_c, pc_l, _rwkv_params(
            layer, rwkv_mu, rwkv_w0, rwkv_w_up, rwkv_a0, rwkv_a_up, rwkv_g_up, rwkv_k_k,
            rwkv_k_a, rwkv_r_k, rwkv_ln_w, rwkv_ln_b))

        x = _outproj(x, ya_l, yb_l, yc_l, wo, lat[2], norm_mlp[layer], lat[4], lat[3],
                     lat[5], w1, w2, norm_final, not ctx_out, layer)
        if ctx_out:
            ctx = _outproj(ctx, ya_c, yb_c, yc_c, wo, cm[2], norm_mlp[layer], cm[4], cm[3],
                           cm[5], w1, w2, norm_final, False, layer)
    return x
```

```python
import functools
import math

import jax
import jax.numpy as jnp
from jax import lax
from jax.experimental import pallas as pl
from jax.experimental.pallas import tpu as pltpu

F32 = jnp.float32
BF16 = jnp.bfloat16
HIGHEST = lax.Precision.HIGHEST

D_MODEL = 1024
N_MOD = 6
EPS = 1e-6
GRID_W = 64

S5_WIDTH = 256
S5_GROUP = 16
S5_GROUPS = 16
S5_STATE = 64
S5_N = S5_GROUPS * S5_STATE
S5_BLOCK = 512

HEAD = 64
HEADS = 6
WIDTH = HEADS * HEAD
GDN_CONV = 5
GDN_TAIL = 128
GDN_COLS_PACKED = 4 * WIDTH + GDN_TAIL
LORA = 64
GATE_LORA = 128
RWKV_COLS = 3 * WIDTH + 2 * LORA + 2 * LORA + GATE_LORA
RWKV_GN_EPS = 64e-5
CHUNK = 64
SUBLANES = 8
BLOCK_CHUNKS = 16
SUB_CHUNKS = 4
CHAIN_LEAD = 2

ROW_TILE = 512


def _dot(a, b, dims=((1,), (0,))):
    return lax.dot_general(a.astype(BF16), b.astype(BF16), (dims, ((), ())),
                           preferred_element_type=F32)


def _dot_f32(a, b, dims=((1,), (0,))):
    return lax.dot_general(a, b, (dims, ((), ())), precision=HIGHEST,
                           preferred_element_type=F32)


_TN = ((0,), (0,))


def _sigmoid(x):
    return 0.5 + 0.5 * jnp.tanh(0.5 * x)


def _silu(x):
    h = 0.5 * x
    return h + h * jnp.tanh(h)


def _softplus(x):
    return jnp.maximum(x, 0.0) + jnp.log1p(jnp.exp(-jnp.abs(x)))


def _bmm(a, b, ta=False, tb=False):
    dims = (((1 if ta else 2,), (2 if tb else 1,)), ((0,), (0,)))
    return lax.dot_general(a.astype(BF16), b.astype(BF16), dims, preferred_element_type=F32)


def _tri_mask(rev, strict):
    shape = (1, CHUNK, CHUNK)
    r = lax.broadcasted_iota(jnp.int32, shape, 1)
    c = lax.broadcasted_iota(jnp.int32, shape, 2)
    if rev:
        return (c > r) if strict else (c >= r)
    return (c < r) if strict else (c <= r)


def _chunk_cumsum_matrix(n, rev):
    r = lax.broadcasted_iota(jnp.int32, (n, n), 0)
    c = lax.broadcasted_iota(jnp.int32, (n, n), 1)
    shift = int(math.log2(CHUNK))
    same = (r >> shift) == (c >> shift)
    return jnp.where(same & ((c >= r) if rev else (c <= r)), 1.0, 0.0).astype(F32)


def _inv_unit_tri(a):
    n = a.shape[-1]
    r = lax.broadcasted_iota(jnp.int32, (1, n, n), 1)
    c = lax.broadcasted_iota(jnp.int32, (1, n, n), 2)
    inv = jnp.where(r == c, 1.0, 0.0).astype(F32) - jnp.where((r >> 1) == (c >> 1), a, 0.0)
    for lvl in range(1, int(math.log2(n))):
        couple = ((r >> (lvl + 1)) == (c >> (lvl + 1))) & ((r >> lvl) != (c >> lvl))
        half = _bmm(inv, jnp.where(couple, a, 0.0))
        yield
        inv = inv - _bmm(half, inv)
        yield
    return inv


def _run(gen):
    while True:
        try:
            next(gen)
        except StopIteration as stop:
            return stop.value


def _interleave(main, side, lead):
    results = {}
    live = {"main": main, "side": side}
    turn = 0
    while live:
        for name in ("main", "side"):
            if name in live and not (name == "side" and turn < lead and "main" in live):
                try:
                    next(live[name])
                except StopIteration as stop:
                    results[name] = stop.value
                    del live[name]
        turn += 1
    return results["main"], results["side"]


def _split_heads(x):
    nc = x.shape[0] // CHUNK
    return jnp.stack([x[ci * CHUNK:(ci + 1) * CHUNK, h * HEAD:(h + 1) * HEAD]
                      for ci in range(nc) for h in range(HEADS)], axis=0)


def _split_cols(x, first):
    nc = x.shape[0] // CHUNK
    return jnp.stack([x[ci * CHUNK:(ci + 1) * CHUNK, first + h:first + h + 1]
                      for ci in range(nc) for h in range(HEADS)], axis=0)


def _chunk_edge_rows(x, rev):
    nc = x.shape[0] // CHUNK
    edge = 0 if rev else CHUNK - 1
    return jnp.concatenate(
        [jnp.broadcast_to(x[ci * CHUNK + edge:ci * CHUNK + edge + 1], (CHUNK, x.shape[1]))
         for ci in range(nc)], axis=0)


def _head_sum(x, ones_ref):
    return _dot(x, ones_ref[...])


def _halo_rows(ref, keep):
    return jnp.where(keep, ref[0], 0.0)


def _chunk_index(j, n_chunks, rev):
    return (n_chunks - 1 - j) if rev else j


def _cparams(sem):
    return pltpu.CompilerParams(dimension_semantics=sem)


def _mods_kernel(c_ref, w_ref, b_ref, o_ref):
    o_ref[0] = _dot_f32(_silu(c_ref[...]), w_ref[0]) + b_ref[0]


def _mods(cvec, mod_w, mod_b):
    depth = mod_w.shape[0]
    rows = cvec.shape[0]
    n_tile = 1536
    return pl.pallas_call(
        _mods_kernel,
        grid=(depth, N_MOD * D_MODEL // n_tile),
        in_specs=[
            pl.BlockSpec((rows, D_MODEL), lambda l, n: (0, 0)),
            pl.BlockSpec((1, D_MODEL, n_tile), lambda l, n: (l, 0, n)),
            pl.BlockSpec((1, 1, n_tile), lambda l, n: (l, 0, n)),
        ],
        out_specs=pl.BlockSpec((1, rows, n_tile), lambda l, n: (l, 0, n)),
        out_shape=jax.ShapeDtypeStruct((depth, rows, N_MOD * D_MODEL), F32),
        compiler_params=_cparams(("arbitrary", "arbitrary")),
        name="mods",
    )(cvec, mod_w, mod_b.reshape(depth, 1, N_MOD * D_MODEL))


def _inproj_kernel(x_ref, g_ref, sc_ref, sh_ref, w_ref, oa_ref, ob_ref, oc_ref):
    x = x_ref[0]
    xn = x * lax.rsqrt(jnp.mean(x * x, axis=-1, keepdims=True) + EPS) * g_ref[...]
    xm = xn * (1.0 + sc_ref[0, 0, 0]) + sh_ref[0, 0, 0]
    p = _dot(xm, w_ref[0])
    oa_ref[0] = p[:, :S5_WIDTH]
    ob_ref[0] = p[:, S5_WIDTH:S5_WIDTH + GDN_COLS_PACKED]
    oc_ref[0] = p[:, S5_WIDTH + GDN_COLS_PACKED:]


def _mod_spec(layer, fixed_row, k):
    def index(i, t):
        return (layer, i if fixed_row is None else fixed_row, k, 0, 0)

    return pl.BlockSpec((1, 1, 1, 1, D_MODEL), index)


def _inproj(x, gain, mods, fixed_row, w_packed, layer):
    b, l, _ = x.shape
    tm = min(ROW_TILE, l)
    cols = w_packed.shape[2]
    return pl.pallas_call(
        _inproj_kernel,
        grid=(b, l // tm),
        in_specs=[
            pl.BlockSpec((1, tm, D_MODEL), lambda i, t: (i, t, 0)),
            pl.BlockSpec((1, D_MODEL), lambda i, t: (0, 0)),
            _mod_spec(layer, fixed_row, 1), _mod_spec(layer, fixed_row, 0),
            pl.BlockSpec((1, D_MODEL, cols), lambda i, t: (layer, 0, 0),
                         pipeline_mode=pl.Buffered(1)),
        ],
        out_specs=[
            pl.BlockSpec((1, tm, S5_WIDTH), lambda i, t: (i, t, 0)),
            pl.BlockSpec((1, tm, GDN_COLS_PACKED), lambda i, t: (i, t, 0)),
            pl.BlockSpec((1, tm, RWKV_COLS), lambda i, t: (i, t, 0)),
        ],
        out_shape=[
            jax.ShapeDtypeStruct((b, l, S5_WIDTH), F32),
            jax.ShapeDtypeStruct((b, l, GDN_COLS_PACKED), F32),
            jax.ShapeDtypeStruct((b, l, RWKV_COLS), F32),
        ],
        compiler_params=_cparams(("parallel", "arbitrary")),
        name="inproj",
    )(x, gain.reshape(1, D_MODEL), mods, mods, w_packed)


def _outproj_kernel(x_ref, ya_ref, yb_ref, yc_ref, wo_ref, gt1_ref,
                    g_ref, sc_ref, sh_ref, gt2_ref, w1_ref, w2_ref, gf_ref, o_ref, *, final):
    o_b = S5_WIDTH
    o_c = S5_WIDTH + WIDTH
    mix = (_dot(ya_ref[0], wo_ref[0, :o_b]) + _dot(yb_ref[0], wo_ref[0, o_b:o_c])
           + _dot(yc_ref[0], wo_ref[0, o_c:]))
    x1 = x_ref[0] + gt1_ref[0, 0, 0] * mix
    h = x1 * lax.rsqrt(jnp.mean(x1 * x1, axis=-1, keepdims=True) + EPS) * g_ref[...]
    h = h * (1.0 + sc_ref[0, 0, 0]) + sh_ref[0, 0, 0]
    a = jnp.maximum(_dot(h, w1_ref[0]), 0.0)
    x2 = x1 + gt2_ref[0, 0, 0] * _dot(a * a, w2_ref[0])
    if final:
        x2 = x2 * lax.rsqrt(jnp.mean(x2 * x2, axis=-1, keepdims=True) + EPS) * gf_ref[...]
    o_ref[0] = x2


def _outproj(x, ya, yb, yc, wo, mods, fixed_row, gain, w1, w2, gain_final, final, layer):
    b, l, _ = x.shape
    tm = min(ROW_TILE, l)
    row = pl.BlockSpec((1, D_MODEL), lambda i, t: (0, 0))

    def mod(k):
        return _mod_spec(layer, fixed_row, k)

    def tok(width):
        return pl.BlockSpec((1, tm, width), lambda i, t: (i, t, 0))

    def stacked(arr):
        return pl.BlockSpec((1,) + arr.shape[1:], lambda i, t: (layer, 0, 0),
                            pipeline_mode=pl.Buffered(1))

    return pl.pallas_call(
        functools.partial(_outproj_kernel, final=final),
        grid=(b, l // tm),
        in_specs=[tok(D_MODEL), tok(S5_WIDTH), tok(WIDTH), tok(WIDTH),
                  stacked(wo), mod(2), row, mod(4), mod(3), mod(5),
                  stacked(w1), stacked(w2), row],
        out_specs=tok(D_MODEL),
        out_shape=jax.ShapeDtypeStruct((b, l, D_MODEL), F32),
        compiler_params=_cparams(("parallel", "arbitrary")),
        name="outproj_mlp",
    )(x, ya, yb, yc, wo, mods, gain.reshape(1, D_MODEL), mods, mods, mods, w1, w2,
      gain_final.reshape(1, D_MODEL))


def _cmul(a, b):
    return a[0] * b[0] - a[1] * b[1], a[0] * b[1] + a[1] * b[0]


def _s5_kernel(*refs, rev, readout, n_chunks):
    (u_ref, are_ref, aim_ref, ldt_ref, bre_ref, bim_ref, cre_ref, cim_ref,
     h0re_ref, h0im_ref) = refs[:10]
    if readout:
        d_ref, gw_ref, gb_ref, yf_ref = refs[10:14]
        y_ref, hre_ref, him_ref = refs[14:17]
        wre, wim, abt, pw, carry, perm_ref = refs[17:]
    else:
        y_ref, hre_ref, him_ref = refs[10:13]
        wre, wim, abt, pw, carry, perm_ref = refs[13:]
    t = u_ref.shape[1]
    seg = t // SUBLANES
    j = pl.program_id(1)
    tile = (SUBLANES, S5_N)

    @pl.when(j == 0)
    def _seed_carry():
        carry[0:1, :] = h0re_ref[0]
        carry[1:2, :] = h0im_ref[0]

    @pl.when((pl.program_id(0) == 0) & (j == 0))
    def _discretise():
        dt = jnp.exp(ldt_ref[...])
        a_re = are_ref[...]
        a_im = aim_ref[...]
        mag = jnp.exp(a_re * dt)
        ab = (mag * jnp.cos(a_im * dt), mag * jnp.sin(a_im * dt))
        den = a_re * a_re + a_im * a_im
        xr = ab[0] - 1.0
        cr = (xr * a_re + ab[1] * a_im) / den
        ci = (ab[1] * a_re - xr * a_im) / den
        wre[...] = (bre_ref[...] * cr - bim_ref[...] * ci).astype(BF16)
        wim[...] = (bre_ref[...] * ci + bim_ref[...] * cr).astype(BF16)
        abt[0] = jnp.broadcast_to(ab[0], tile)
        abt[1] = jnp.broadcast_to(ab[1], tile)
        power = ab
        for k in range(seg):
            pw[0, k] = jnp.broadcast_to(power[0], tile)
            pw[1, k] = jnp.broadcast_to(power[1], tile)
            if k + 1 < seg:
                power = _cmul(power, ab)
        m_idx = lax.broadcasted_iota(jnp.int32, (t, t), 0)
        n_idx = lax.broadcasted_iota(jnp.int32, (t, t), 1)
        src = seg * (m_idx & (SUBLANES - 1)) + (m_idx >> int(math.log2(SUBLANES)))
        perm_ref[...] = jnp.where(n_idx == src, 1.0, 0.0).astype(BF16)

    u = u_ref[0]
    perm = perm_ref[...]
    u_p = _dot(perm, u)
    bur = _dot(u_p, wre[...])
    bui = _dot(u_p, wim[...])
    steps = range(seg - 1, -1, -1) if rev else range(seg)
    a_r = abt[0]
    a_i = abt[1]
    hr = jnp.zeros(tile, F32)
    hi = jnp.zeros(tile, F32)
    local = [None] * seg
    for k in steps:
        rows = slice(k * SUBLANES, (k + 1) * SUBLANES)
        hr, hi = a_r * hr - a_i * hi + bur[rows], a_r * hi + a_i * hr + bui[rows]
        local[k] = (hr, hi)
    a_seg = (pw[0, seg - 1][0:1], pw[1, seg - 1][0:1])
    c = (carry[0:1, :], carry[1:2, :])
    row = lax.broadcasted_iota(jnp.int32, tile, 0)
    c_re = jnp.zeros(tile, F32)
    c_im = jnp.zeros(tile, F32)
    for r in (range(SUBLANES - 1, -1, -1) if rev else range(SUBLANES)):
        c_re = jnp.where(row == r, c[0], c_re)
        c_im = jnp.where(row == r, c[1], c_im)
        c = (hr[r:r + 1] + a_seg[0] * c[0] - a_seg[1] * c[1],
             hi[r:r + 1] + a_seg[0] * c[1] + a_seg[1] * c[0])
    carry[0:1, :] = c[0]
    carry[1:2, :] = c[1]
    h_re = [None] * seg
    h_im = [None] * seg
    for n, k in enumerate(steps):
        h_re[k] = local[k][0] + pw[0, n] * c_re - pw[1, n] * c_im
        h_im[k] = local[k][1] + pw[0, n] * c_im + pw[1, n] * c_re
    y_p = (_dot(jnp.concatenate(h_re, axis=0), cre_ref[...])
           - _dot(jnp.concatenate(h_im, axis=0), cim_ref[...]))
    y_hi = y_p.astype(BF16)
    y = _dot(perm, y_hi, _TN) + _dot(perm, y_p - y_hi.astype(F32), _TN)
    c_re, c_im = c
    if readout:
        y = y + yf_ref[0] + d_ref[...] * u
        z = 0.5 * y * (1.0 + jnp.tanh(math.sqrt(2.0 / math.pi) * (y + 0.044715 * y * y * y)))
        y = z * _sigmoid(_dot(z, gw_ref[...]) + gb_ref[...])
    y_ref[0] = y

    @pl.when(j == n_chunks - 1)
    def _emit_state():
        hre_ref[0] = c_re
        him_ref[0] = c_im


def _s5_call(u, prm, d, h0, rev, readout, yf=None):
    b, l, _ = u.shape
    t = min(S5_BLOCK, l)
    n_chunks = l // t
    tok = pl.BlockSpec((1, t, S5_WIDTH), lambda i, j: (i, _chunk_index(j, n_chunks, rev), 0))
    row_n = pl.BlockSpec((1, S5_N), lambda i, j: (0, 0))
    st = pl.BlockSpec((1, 1, S5_N), lambda i, j: (i, 0, 0))

    def full(arr):
        return pl.BlockSpec(arr.shape, lambda i, j: (0,) * arr.ndim)

    args = [u, prm["a_re"][d], prm["a_im"][d], prm["log_dt"][d], prm["b_re"], prm["b_im"],
            prm["c_re"], prm["c_im"], h0[0], h0[1]]
    specs = [tok, row_n, row_n, row_n, full(prm["b_re"]), full(prm["b_im"]),
             full(prm["c_re"]), full(prm["c_im"]), st, st]
    if readout:
        args += [prm["d"], prm["glu_w"], prm["glu_b"], yf]
        specs += [full(prm["d"]), full(prm["glu_w"]), full(prm["glu_b"]), tok]
    y, hre, him = pl.pallas_call(
        functools.partial(_s5_kernel, rev=rev, readout=readout, n_chunks=n_chunks),
        grid=(b, n_chunks),
        in_specs=specs,
        out_specs=[tok, st, st],
        out_shape=[jax.ShapeDtypeStruct((b, l, S5_WIDTH), F32),
                   jax.ShapeDtypeStruct((b, 1, S5_N), F32),
                   jax.ShapeDtypeStruct((b, 1, S5_N), F32)],
        scratch_shapes=[pltpu.VMEM((S5_WIDTH, S5_N), BF16), pltpu.VMEM((S5_WIDTH, S5_N), BF16),
                        pltpu.VMEM((2, SUBLANES, S5_N), F32),
                        pltpu.VMEM((2, t // SUBLANES, SUBLANES, S5_N), F32),
                        pltpu.VMEM((2, S5_N), F32), pltpu.VMEM((t, t), BF16)],
        compiler_params=_cparams(("arbitrary", "arbitrary")),
        name="s5_bwd" if rev else "s5_fwd",
    )(*args)
    return y, (hre, him)


def _s5_mixer(u_ctx, u_lat, prm):
    b = u_lat.shape[0]
    zero = (jnp.zeros((b, 1, S5_N), F32), jnp.zeros((b, 1, S5_N), F32))
    yc_f, hc_f = _s5_call(u_ctx, prm, 0, zero, False, False)
    y_ctx, hc_b = _s5_call(u_ctx, prm, 1, zero, True, True, yc_f)
    yl_f, _ = _s5_call(u_lat, prm, 0, hc_f, False, False)
    y_lat, _ = _s5_call(u_lat, prm, 1, hc_b, True, True, yl_f)
    return y_ctx, y_lat


def _block_order(nc, rev):
    return range(nc - 1, -1, -1) if rev else range(nc)


def _store_heads(buf, ci, o):
    for h in range(HEADS):
        buf[ci * CHUNK:(ci + 1) * CHUNK, h * HEAD:(h + 1) * HEAD] = o[h]


def _gdn_kernel(*refs, rev, readout, n_blocks, d):
    pp_ref, pc_ref, pn_ref, cw_ref, alog_ref, dtb_ref, ones_ref, s0_ref = refs[:8]
    if readout:
        nw_ref, of_ref = refs[8:10]
        y_ref, so_ref = refs[10:12]
        state, obuf = refs[12:]
    else:
        y_ref, so_ref = refs[8:10]
        state, obuf = refs[10:]
    bt = pc_ref.shape[1]
    j = pl.program_id(1)
    c = _chunk_index(j, n_blocks, rev)

    @pl.when(j == 0)
    def _load_state():
        state[...] = s0_ref[0]

    halo = pp_ref.shape[1]
    ext = jnp.concatenate([_halo_rows(pp_ref, c > 0), pc_ref[0],
                           _halo_rows(pn_ref, c < n_blocks - 1)], axis=0)
    incl = _tri_mask(rev, False)
    strict = _tri_mask(rev, True)
    sub = min(SUB_CHUNKS * CHUNK, bt)
    nc = sub // CHUNK
    cum = _chunk_cumsum_matrix(sub, rev)
    half = GDN_CONV // 2
    first_g = 2 * HEADS + d * HEADS
    edge = 0 if rev else CHUNK - 1

    def prepare(si):
        lo = si * sub
        seg = ext[lo:lo + sub + 2 * halo]
        x = seg[halo:halo + sub]
        qkv = seg[:, :3 * WIDTH]
        acc = cw_ref[half:half + 1, :] * x[:, :3 * WIDTH]
        for off in range(-half, half + 1):
            if off != 0:
                tap = pltpu.roll(qkv, (-off) % (sub + 2 * halo), 0)[halo:halo + sub]
                acc = acc + cw_ref[half + off:half + off + 1, :] * tap
                yield
        act = _silu(acc)
        q = act[:, :WIDTH]
        k = act[:, WIDTH:2 * WIDTH]
        v = act[:, 2 * WIDTH:]
        q2 = q * q
        k2 = k * k
        tail = x[:, 4 * WIDTH:]
        beta_all = _sigmoid(tail)
        g_all = -jnp.exp(alog_ref[...]) * _softplus(tail + dtb_ref[...])
        gated = nw_ref[...] * _silu(x[:, 3 * WIDTH:4 * WIDTH]) if readout else None
        yield
        yield
        q_ssq = _head_sum(q2, ones_ref)
        k_ssq = _head_sum(k2, ones_ref)
        gcs = _dot_f32(cum, g_all)
        yield
        yield
        q = q * (lax.rsqrt(q_ssq + EPS) * HEAD ** -0.5)
        k = k * lax.rsqrt(k_ssq + EPS)
        gcs_t = gcs.T
        gc = _split_cols(gcs, first_g)
        gr = jnp.stack([gcs_t[first_g + h:first_g + h + 1, ci * CHUNK:(ci + 1) * CHUNK]
                        for ci in range(nc) for h in range(HEADS)], axis=0)
        beta = _split_cols(beta_all, d * HEADS)
        yield
        decay = jnp.where(incl, jnp.exp(jnp.minimum(gc - gr, 0.0)), 0.0)
        yield
        qs = _split_heads(q)
        ks = _split_heads(k)
        yield
        vs = _split_heads(v)
        kb = ks * beta
        eg = jnp.exp(gc)
        g_last = gc[:, edge:edge + 1]
        yield
        return dict(lhs=jnp.concatenate([kb, qs], axis=1), ks=ks, decay=decay,
                    vw=jnp.concatenate([vs * beta, kb * eg], axis=2), q_eg=qs * eg,
                    kd=ks * jnp.exp(g_last - gc), dec_last=jnp.exp(g_last), gated=gated)

    def chain(si, op, s):
        rows = slice(si * sub, (si + 1) * sub)
        big = _bmm(op["lhs"], op["ks"], tb=True)
        yield
        m = jnp.where(strict, big[:, :CHUNK] * op["decay"], 0.0)
        attn = big[:, CHUNK:] * op["decay"]
        yield
        tinv = yield from _inv_unit_tri(m)
        uw = _bmm(tinv, op["vw"])
        yield
        au = _bmm(attn, uw)
        yield
        q_eff = op["q_eg"] - au[:, :, HEAD:]
        y_in = au[:, :, :HEAD]
        kdu = _bmm(op["kd"], uw, ta=True)
        yield
        first = si * nc
        for ci in _block_order(nc, rev):
            sl = slice(ci * HEADS, (ci + 1) * HEADS)
            _store_heads(obuf, first + ci, _bmm(q_eff[sl], s) + y_in[sl])
            s = s * op["dec_last"][sl] - _bmm(kdu[sl, :, HEAD:], s) + kdu[sl, :, :HEAD]
            yield
        o = obuf[rows]
        if readout:
            o = o + of_ref[0, rows]
            scale = lax.rsqrt(_head_sum(o * o, ones_ref) * (1.0 / HEAD) + EPS)
            o = o * scale * op["gated"]
        y_ref[0, rows] = o
        return s

    order = list(_block_order(bt // sub, rev))
    s = state[...]
    op = _run(prepare(order[0]))
    for n, si in enumerate(order):
        if n + 1 < len(order):
            s, op = _interleave(chain(si, op, s), prepare(order[n + 1]), CHAIN_LEAD)
        else:
            s = _run(chain(si, op, s))
    state[...] = s

    @pl.when(j == n_blocks - 1)
    def _emit_state():
        so_ref[0] = state[...]


def _block_rows(l, chunks):
    return min(chunks * CHUNK, l)


def _halo_specs(width, bt, n_blocks, rev, halo):
    per_block = bt // halo
    last = n_blocks * per_block - 1

    def cur(i, j):
        return (i, _chunk_index(j, n_blocks, rev), 0)

    def prev(i, j):
        return (i, jnp.maximum(_chunk_index(j, n_blocks, rev) * per_block - 1, 0), 0)

    def nxt(i, j):
        return (i, jnp.minimum((_chunk_index(j, n_blocks, rev) + 1) * per_block, last), 0)

    return [pl.BlockSpec((1, halo, width), prev), pl.BlockSpec((1, bt, width), cur),
            pl.BlockSpec((1, halo, width), nxt)]


def _gdn_call(p, prm, d, s0, rev, readout, of=None):
    b, l, _ = p.shape
    bt = _block_rows(l, BLOCK_CHUNKS)
    n_blocks = l // bt
    tok = pl.BlockSpec((1, bt, WIDTH), lambda i, j: (i, _chunk_index(j, n_blocks, rev), 0))
    st = pl.BlockSpec((1, HEADS, HEAD, HEAD), lambda i, j: (i, 0, 0, 0))

    def full(arr):
        return pl.BlockSpec(arr.shape, lambda i, j: (0,) * arr.ndim)

    args = [p, p, p, prm["conv"], prm["alog"][d], prm["dtb"][d], prm["ones"], s0]
    specs = _halo_specs(GDN_COLS_PACKED, bt, n_blocks, rev, SUBLANES) + [
        full(prm["conv"]), full(prm["alog"][d]), full(prm["dtb"][d]), full(prm["ones"]), st]
    if readout:
        args += [prm["norm"], of]
        specs += [full(prm["norm"]), tok]
    return pl.pallas_call(
        functools.partial(_gdn_kernel, rev=rev, readout=readout, n_blocks=n_blocks, d=d),
        grid=(b, n_blocks),
        in_specs=specs,
        out_specs=[tok, st],
        out_shape=[jax.ShapeDtypeStruct((b, l, WIDTH), F32),
                   jax.ShapeDtypeStruct((b, HEADS, HEAD, HEAD), F32)],
        scratch_shapes=[pltpu.VMEM((HEADS, HEAD, HEAD), F32), pltpu.VMEM((bt, WIDTH), F32)],
        compiler_params=_cparams(("parallel", "arbitrary")),
        name="gdn_bwd" if rev else "gdn_fwd",
    )(*args)


def _gdn_mixer(p_ctx, p_lat, prm):
    b = p_lat.shape[0]
    zero = jnp.zeros((b, HEADS, HEAD, HEAD), F32)
    oc_f, sc_f = _gdn_call(p_ctx, prm, 0, zero, False, False)
    y_ctx, sc_b = _gdn_call(p_ctx, prm, 1, zero, True, True, oc_f)
    ol_f, _ = _gdn_call(p_lat, prm, 0, sc_f, False, False)
    y_lat, _ = _gdn_call(p_lat, prm, 1, sc_b, True, True, ol_f)
    return y_ctx, y_lat


def _rwkv_kernel(*refs, rev, readout, grid_shift, n_blocks):
    (pp_ref, pc_ref, pn_ref, mu_ref, w0_ref, wup_ref, a0_ref, aup_ref, kk_ref, ka_ref, rk_ref,
     ones_ref, s0_ref) = refs[:13]
    if readout:
        gup_ref, lnw_ref, lnb_ref, yf_ref, bf_ref = refs[13:18]
        y_ref, so_ref = refs[18:20]
        state, obuf = refs[20:]
    else:
        y_ref, bon_ref, so_ref = refs[13:16]
        state, obuf = refs[16:]
    bt = pc_ref.shape[1]
    j = pl.program_id(1)
    c = _chunk_index(j, n_blocks, rev)

    @pl.when(j == 0)
    def _load_state():
        state[...] = s0_ref[0]

    halo = pp_ref.shape[1]
    ext = jnp.concatenate([_halo_rows(pp_ref, c > 0), pc_ref[0],
                           _halo_rows(pn_ref, c < n_blocks - 1)], axis=0)
    incl = _tri_mask(rev, False)
    strict = _tri_mask(rev, True)
    sub = min(SUB_CHUNKS * CHUNK, bt)
    cum = _chunk_cumsum_matrix(sub, rev)

    def prepare(si):
        lo = si * sub
        seg = ext[lo:lo + sub + 2 * halo]
        x = seg[halo:halo + sub]
        lane = lax.broadcasted_iota(jnp.int32, x.shape, 1)
        if grid_shift:
            col = lax.broadcasted_iota(jnp.int32, x.shape, 0) & (GRID_W - 1)
            left = jnp.where(col == 0, 0.0, pltpu.roll(x, 1, 0))
            right = jnp.where(col == GRID_W - 1, 0.0, pltpu.roll(x, sub - 1, 0))
            cm = lane & 3
            shifted = jnp.where(cm == 0, left, jnp.where(
                cm == 1, right, jnp.where(cm == 2, seg[:sub], seg[2 * halo:])))
        else:
            shifted = jnp.where((lane & 1) == 0, pltpu.roll(seg, 1, 0)[halo:halo + sub],
                                pltpu.roll(seg, sub + 2 * halo - 1, 0)[halo:halo + sub])
        yield
        pm = x + (shifted - x) * mu_ref[...]
        r = pm[:, :WIDTH]
        k = pm[:, WIDTH:2 * WIDTH]
        v = pm[:, 2 * WIDTH:3 * WIDTH]
        o1 = 3 * WIDTH
        w_dn = pm[:, o1:o1 + 2 * LORA]
        a_dn = pm[:, o1 + 2 * LORA:o1 + 4 * LORA]
        g_dn = pm[:, o1 + 4 * LORA:]
        w_act = jnp.tanh(w_dn)
        g_act = _sigmoid(g_dn)
        kkr = k * kk_ref[...]
        kkr2 = kkr * kkr
        yield
        yield
        w_lora = _dot(w_act, wup_ref[...])
        a_lora = _dot(a_dn, aup_ref[...])
        kk_ssq = _head_sum(kkr2, ones_ref)
        gate = _dot(g_act, gup_ref[...]) if readout else None
        yield
        yield
        w_log = -_softplus(-(w0_ref[...] + w_lora)) - 0.5
        lw = -jnp.exp(w_log)
        iclr = _sigmoid(a0_ref[...] + a_lora)
        kk = kkr * lax.rsqrt(kk_ssq + EPS)
        k_dir = k * (1.0 + (iclr - 1.0) * ka_ref[...])
        b_dir = kk * iclr
        rkr = r * k_dir * rk_ref[...]
        yield
        yield
        g_inc = _dot_f32(cum, lw)
        bonus = _head_sum(rkr, ones_ref) * v
        yield
        yield
        g_end = _chunk_edge_rows(g_inc, rev)
        e_neg = jnp.exp(-g_inc)
        e_end = jnp.exp(g_end - g_inc)
        yield
        lhs = jnp.concatenate([_split_heads(kk * jnp.exp(g_inc - lw)),
                               _split_heads(r * jnp.exp(g_inc))], axis=1)
        yield
        rhs = jnp.concatenate([_split_heads(b_dir * e_neg), _split_heads(k_dir * e_neg)], axis=1)
        yield
        b_eh = _split_heads(b_dir * e_end)
        yield
        k_eh = _split_heads(k_dir * e_end)
        yield
        v_h = _split_heads(v)
        dec_end = _split_heads(jnp.exp(g_end))[:, :1, :]
        return dict(lhs=lhs, rhs=rhs, b_eh=b_eh, k_eh=k_eh, v_h=v_h, dec_end=dec_end,
                    bonus=bonus, gate=gate)

    def chain(si, op, s):
        rows = slice(si * sub, (si + 1) * sub)
        kap_h = op["lhs"][:, :CHUNK]
        r_h = op["lhs"][:, CHUNK:]
        v_h = op["v_h"]
        big = _bmm(op["lhs"], op["rhs"], tb=True)
        yield
        a_b = jnp.where(strict, big[:, :CHUNK, :CHUNK], 0.0)
        a_k = jnp.where(strict, big[:, :CHUNK, CHUNK:], 0.0)
        r_b = jnp.where(incl, big[:, CHUNK:, :CHUNK], 0.0)
        r_k = jnp.where(incl, big[:, CHUNK:, CHUNK:], 0.0)
        akv = _bmm(a_k, v_h)
        yield
        tinv = yield from _inv_unit_tri(a_b)
        z = _bmm(tinv, jnp.concatenate([kap_h, akv], axis=2))
        yield
        rz = _bmm(r_b, z)
        yield
        q_eff = r_h - rz[:, :, :HEAD]
        y_in = _bmm(r_k, v_h) - rz[:, :, HEAD:]
        yield
        ztb = _bmm(z, op["b_eh"], ta=True)
        yield
        pt = ztb[:, :HEAD]
        xt = _bmm(v_h, op["k_eh"], ta=True) - ztb[:, HEAD:]
        yield
        first = si * (sub // CHUNK)
        for ci in _block_order(sub // CHUNK, rev):
            sl = slice(ci * HEADS, (ci + 1) * HEADS)
            _store_heads(obuf, first + ci, _bmm(q_eff[sl], s, tb=True) + y_in[sl])
            s = s * op["dec_end"][sl] - _bmm(s, pt[sl]) + xt[sl]
            yield
        y = obuf[rows]
        if readout:
            y = y + yf_ref[0, rows]
            yc = y - _head_sum(y, ones_ref) * (1.0 / HEAD)
            var = _head_sum(yc * yc, ones_ref) * (1.0 / HEAD)
            yn = yc * lax.rsqrt(var + RWKV_GN_EPS) * lnw_ref[...] + lnb_ref[...]
            y_ref[0, rows] = (yn + op["bonus"] + bf_ref[0, rows]) * op["gate"]
        else:
            y_ref[0, rows] = y
            bon_ref[0, rows] = op["bonus"]
        return s

    order = list(_block_order(bt // sub, rev))
    s = state[...]
    op = _run(prepare(order[0]))
    for n, si in enumerate(order):
        if n + 1 < len(order):
            s, op = _interleave(chain(si, op, s), prepare(order[n + 1]), CHAIN_LEAD)
        else:
            s = _run(chain(si, op, s))
    state[...] = s

    @pl.when(j == n_blocks - 1)
    def _emit_state():
        so_ref[0] = state[...]


def _rwkv_call(p, prm, d, s0, rev, readout, grid_shift, yf=None, bf=None):
    b, l, _ = p.shape
    bt = _block_rows(l, BLOCK_CHUNKS)
    n_blocks = l // bt
    tok = pl.BlockSpec((1, bt, WIDTH), lambda i, j: (i, _chunk_index(j, n_blocks, rev), 0))
    st = pl.BlockSpec((1, HEADS, HEAD, HEAD), lambda i, j: (i, 0, 0, 0))

    def full(arr):
        return pl.BlockSpec(arr.shape, lambda i, j: (0,) * arr.ndim)

    small = [prm["mu"], prm["w0"][d], prm["w_up"][d], prm["a0"][d], prm["a_up"][d],
             prm["k_k"], prm["k_a"], prm["r_k"], prm["ones"]]
    args = [p, p, p] + small + [s0]
    halo = GRID_W if grid_shift else SUBLANES
    specs = _halo_specs(RWKV_COLS, bt, n_blocks, rev, halo) + [full(a) for a in small] + [st]
    tok_shape = jax.ShapeDtypeStruct((b, l, WIDTH), F32)
    st_shape = jax.ShapeDtypeStruct((b, HEADS, HEAD, HEAD), F32)
    if readout:
        extra = [prm["g_up"], prm["ln_w"], prm["ln_b"]]
        args += extra + [yf, bf]
        specs += [full(a) for a in extra] + [tok, tok]
        out_specs, out_shape = [tok, st], [tok_shape, st_shape]
    else:
        out_specs, out_shape = [tok, tok, st], [tok_shape, tok_shape, st_shape]
    return pl.pallas_call(
        functools.partial(_rwkv_kernel, rev=rev, readout=readout, grid_shift=grid_shift,
                          n_blocks=n_blocks),
        grid=(b, n_blocks),
        in_specs=specs,
        out_specs=out_specs,
        out_shape=out_shape,
        scratch_shapes=[pltpu.VMEM((HEADS, HEAD, HEAD), F32), pltpu.VMEM((bt, WIDTH), F32)],
        compiler_params=_cparams(("parallel", "arbitrary")),
        name="rwkv_bwd" if rev else "rwkv_fwd",
    )(*args)


def _rwkv_mixer(p_ctx, p_lat, prm):
    b = p_lat.shape[0]
    zero = jnp.zeros((b, HEADS, HEAD, HEAD), F32)
    yc_f, bc_f, sc_f = _rwkv_call(p_ctx, prm, 0, zero, False, False, False)
    y_ctx, sc_b = _rwkv_call(p_ctx, prm, 1, zero, True, True, False, yc_f, bc_f)
    yl_f, bl_f, _ = _rwkv_call(p_lat, prm, 0, sc_f, False, False, True)
    y_lat, _ = _rwkv_call(p_lat, prm, 1, sc_b, True, True, True, yl_f, bl_f)
    return y_ctx, y_lat


def _pack_w_in(w):
    o_rwkv = S5_WIDTH + 4 * WIDTH + 4 * HEADS
    pad = jnp.zeros(w.shape[:-1] + (GDN_TAIL - 4 * HEADS,), w.dtype)
    return jnp.concatenate([w[..., :o_rwkv], pad, w[..., o_rwkv:]], axis=-1)


def _block_diag(blocks):
    g, r, c = blocks.shape
    eye = jnp.eye(g, dtype=blocks.dtype)
    return (eye[:, None, :, None] * blocks[:, :, None, :]).reshape(g * r, g * c)


def _s5_params(layer, b_re, b_im, c_re, c_im, d, a_re, a_im, log_dt, glu_w, glu_b):
    row = lambda v: v.reshape(2, 1, S5_N)
    return {
        "a_re": row(a_re[layer]), "a_im": row(a_im[layer]),
        "log_dt": row(jnp.repeat(log_dt[layer], S5_STATE, axis=-1)),
        "b_re": _block_diag(jnp.swapaxes(b_re[layer], 1, 2)),
        "b_im": _block_diag(jnp.swapaxes(b_im[layer], 1, 2)),
        "c_re": _block_diag(jnp.swapaxes(c_re[layer], 1, 2)),
        "c_im": _block_diag(jnp.swapaxes(c_im[layer], 1, 2)),
        "d": d[layer].reshape(1, S5_WIDTH), "glu_w": glu_w[layer],
        "glu_b": glu_b[layer].reshape(1, S5_WIDTH),
    }


def _head_ones():
    return _block_diag(jnp.ones((HEADS, HEAD, HEAD), BF16))


def _gdn_params(layer, conv, a_log, dt_bias, norm):
    def tail_row(v):
        rows = []
        for d in range(2):
            z = jnp.zeros((GDN_TAIL,), F32)
            rows.append(z.at[2 * HEADS + d * HEADS:2 * HEADS + (d + 1) * HEADS].set(v[d])
                        .reshape(1, GDN_TAIL))
        return rows

    return {"conv": conv[layer], "alog": tail_row(a_log[layer]), "dtb": tail_row(dt_bias[layer]),
            "norm": jnp.tile(norm[layer], HEADS).reshape(1, WIDTH), "ones": _head_ones()}


def _rwkv_params(layer, mu, w0, w_up, a0, a_up, g_up, k_k, k_a, r_k, ln_w, ln_b):
    def lora_pad(w):
        z = jnp.zeros((LORA, WIDTH), F32)
        return [jnp.concatenate([w[0], z], axis=0), jnp.concatenate([z, w[1]], axis=0)]

    row = lambda v: v.reshape(1, -1)
    return {
        "mu": row(mu[layer]), "w0": [row(w0[layer, 0]), row(w0[layer, 1])],
        "w_up": lora_pad(w_up[layer]), "a0": [row(a0[layer, 0]), row(a0[layer, 1])],
        "a_up": lora_pad(a_up[layer]), "g_up": g_up[layer], "k_k": row(k_k[layer]),
        "k_a": row(k_a[layer]), "r_k": row(r_k[layer]), "ln_w": row(ln_w[layer]),
        "ln_b": row(ln_b[layer]), "ones": _head_ones(),
    }


def kernel(x, c, ctx, c_ctx, mod_w, mod_b, norm_mix, norm_mlp, norm_final, w_in, w_out, s5_b_re, s5_b_im, s5_c_re, s5_c_im, s5_d, s5_a_re, s5_a_im, s5_log_dt, s5_glu_w, s5_glu_b, gdn_conv, gdn_a_log, gdn_dt_bias, gdn_norm, rwkv_mu, rwkv_w0, rwkv_w_up, rwkv_a0, rwkv_a_up, rwkv_g_up, rwkv_k_k, rwkv_k_a, rwkv_r_k, rwkv_ln_w, rwkv_ln_b, mlp_w1, mlp_w2):
    depth = mod_w.shape[0]
    batch = x.shape[0]
    x = x.astype(F32)
    ctx = ctx.astype(F32)
    rows = SUBLANES * ((batch + 1 + SUBLANES - 1) // SUBLANES)
    cvec = jnp.zeros((rows, D_MODEL), F32).at[:batch].set(c.astype(F32)).at[batch].set(
        c_ctx.astype(F32))
    mods = _mods(cvec, mod_w, mod_b).reshape(depth, rows, N_MOD, 1, D_MODEL)
    w_packed = _pack_w_in(w_in.astype(BF16))
    wo = w_out.astype(BF16)
    w1 = mlp_w1.astype(BF16)
    w2 = mlp_w2.astype(BF16)

    for layer in range(depth):
        ctx_out = layer < depth - 1
        pa_l, pb_l, pc_l = _inproj(x, norm_mix[layer], mods, None, w_packed, layer)
        pa_c, pb_c, pc_c = _inproj(ctx, norm_mix[layer], mods, batch, w_packed, layer)

        ya_c, ya_l = _s5_mixer(pa_c, pa_l, _s5_params(
            layer, s5_b_re, s5_b_im, s5_c_re, s5_c_im, s5_d, s5_a_re, s5_a_im, s5_log_dt,
            s5_glu_w, s5_glu_b))
        yb_c, yb_l = _gdn_mixer(pb_c, pb_l, _gdn_params(
            layer, gdn_conv, gdn_a_log, gdn_dt_bias, gdn_norm))
        yc_c, yc_l = _rwkv_mixer(pc_c, pc_l, _rwkv_params(
            layer, rwkv_mu, rwkv_w0, rwkv_w_up, rwkv_a0, rwkv_a_up, rwkv_g_up, rwkv_k_k,
            rwkv_k_a, rwkv_r_k, rwkv_ln_w, rwkv_ln_b))

        x = _outproj(x, ya_l, yb_l, yc_l, wo, mods, None, norm_mlp[layer], w1, w2, norm_final,
                     not ctx_out, layer)
        if ctx_out:
            ctx = _outproj(ctx, ya_c, yb_c, yc_c, wo, mods, batch, norm_mlp[layer], w1, w2,
                           norm_final, False, layer)
    return x
```

```python
import functools
import math

import jax
import jax.numpy as jnp
from jax import lax
from jax.experimental import pallas as pl
from jax.experimental.pallas import tpu as pltpu

F32 = jnp.float32
BF16 = jnp.bfloat16
HIGHEST = lax.Precision.HIGHEST

D_MODEL = 1024
N_MOD = 6
EPS = 1e-6
GRID_W = 64

S5_WIDTH = 256
S5_GROUP = 16
S5_GROUPS = 16
S5_STATE = 64
S5_N = S5_GROUPS * S5_STATE
S5_BLOCK = 512

HEAD = 64
HEADS = 6
WIDTH = HEADS * HEAD
GDN_CONV = 5
GDN_TAIL = 128
GDN_COLS_PACKED = 4 * WIDTH + GDN_TAIL
LORA = 64
GATE_LORA = 128
RWKV_COLS = 3 * WIDTH + 2 * LORA + 2 * LORA + GATE_LORA
RWKV_GN_EPS = 64e-5
CHUNK = 64
SUBLANES = 8
BLOCK_CHUNKS = 16
SUB_CHUNKS = 4
CHAIN_LEAD = 2

ROW_TILE = 512


def _dot(a, b, dims=((1,), (0,))):
    return lax.dot_general(a.astype(BF16), b.astype(BF16), (dims, ((), ())),
                           preferred_element_type=F32)


def _dot_f32(a, b, dims=((1,), (0,))):
    return lax.dot_general(a, b, (dims, ((), ())), precision=HIGHEST,
                           preferred_element_type=F32)


_TN = ((0,), (0,))


def _dot_select(m01, x):
    hi = x.astype(BF16)
    rest = x - hi.astype(F32)
    mid = rest.astype(BF16)
    lo = (rest - mid.astype(F32)).astype(BF16)
    m = m01.astype(BF16)
    return _dot(m, hi) + _dot(m, mid) + _dot(m, lo)


def _sigmoid(x):
    return 0.5 + 0.5 * jnp.tanh(0.5 * x)


def _silu(x):
    h = 0.5 * x
    return h + h * jnp.tanh(h)


def _softplus(x):
    return jnp.maximum(x, 0.0) + jnp.log1p(jnp.exp(-jnp.abs(x)))


def _bmm(a, b, ta=False, tb=False):
    dims = (((1 if ta else 2,), (2 if tb else 1,)), ((0,), (0,)))
    return lax.dot_general(a.astype(BF16), b.astype(BF16), dims, preferred_element_type=F32)


def _tri_mask(rev, strict):
    shape = (1, CHUNK, CHUNK)
    r = lax.broadcasted_iota(jnp.int32, shape, 1)
    c = lax.broadcasted_iota(jnp.int32, shape, 2)
    if rev:
        return (c > r) if strict else (c >= r)
    return (c < r) if strict else (c <= r)


def _chunk_cumsum_matrix(n, rev):
    r = lax.broadcasted_iota(jnp.int32, (n, n), 0)
    c = lax.broadcasted_iota(jnp.int32, (n, n), 1)
    shift = int(math.log2(CHUNK))
    same = (r >> shift) == (c >> shift)
    return jnp.where(same & ((c >= r) if rev else (c <= r)), 1.0, 0.0).astype(F32)


def _inv_unit_tri(a):
    n = a.shape[-1]
    r = lax.broadcasted_iota(jnp.int32, (1, n, n), 1)
    c = lax.broadcasted_iota(jnp.int32, (1, n, n), 2)
    inv = jnp.where(r == c, 1.0, 0.0).astype(F32) - jnp.where((r >> 1) == (c >> 1), a, 0.0)
    for lvl in range(1, int(math.log2(n))):
        couple = ((r >> (lvl + 1)) == (c >> (lvl + 1))) & ((r >> lvl) != (c >> lvl))
        half = _bmm(inv, jnp.where(couple, a, 0.0))
        yield
        inv = inv - _bmm(half, inv)
        yield
    return inv


def _run(gen):
    while True:
        try:
            next(gen)
        except StopIteration as stop:
            return stop.value


def _interleave(main, side, lead):
    results = {}
    live = {"main": main, "side": side}
    turn = 0
    while live:
        for name in ("main", "side"):
            if name in live and not (name == "side" and turn < lead and "main" in live):
                try:
                    next(live[name])
                except StopIteration as stop:
                    results[name] = stop.value
                    del live[name]
        turn += 1
    return results["main"], results["side"]


def _split_heads(x):
    nc = x.shape[0] // CHUNK
    return jnp.stack([x[ci * CHUNK:(ci + 1) * CHUNK, h * HEAD:(h + 1) * HEAD]
                      for ci in range(nc) for h in range(HEADS)], axis=0)


def _split_cols(x, first):
    nc = x.shape[0] // CHUNK
    return jnp.stack([x[ci * CHUNK:(ci + 1) * CHUNK, first + h:first + h + 1]
                      for ci in range(nc) for h in range(HEADS)], axis=0)


def _chunk_edge_rows(x, rev):
    nc = x.shape[0] // CHUNK
    edge = 0 if rev else CHUNK - 1
    return jnp.concatenate(
        [jnp.broadcast_to(x[ci * CHUNK + edge:ci * CHUNK + edge + 1], (CHUNK, x.shape[1]))
         for ci in range(nc)], axis=0)


def _head_sum(x, ones_ref):
    return _dot(x, ones_ref[...])


def _halo_rows(ref, keep):
    return jnp.where(keep, ref[0], 0.0)


def _chunk_index(j, n_chunks, rev):
    return (n_chunks - 1 - j) if rev else j


def _cparams(sem):
    return pltpu.CompilerParams(dimension_semantics=sem)


def _mods_kernel(c_ref, w_ref, b_ref, o_ref):
    o_ref[0] = _dot_f32(_silu(c_ref[...]), w_ref[0]) + b_ref[0]


def _mods(cvec, mod_w, mod_b):
    depth = mod_w.shape[0]
    rows = cvec.shape[0]
    n_tile = 1536
    return pl.pallas_call(
        _mods_kernel,
        grid=(depth, N_MOD * D_MODEL // n_tile),
        in_specs=[
            pl.BlockSpec((rows, D_MODEL), lambda l, n: (0, 0)),
            pl.BlockSpec((1, D_MODEL, n_tile), lambda l, n: (l, 0, n)),
            pl.BlockSpec((1, 1, n_tile), lambda l, n: (l, 0, n)),
        ],
        out_specs=pl.BlockSpec((1, rows, n_tile), lambda l, n: (l, 0, n)),
        out_shape=jax.ShapeDtypeStruct((depth, rows, N_MOD * D_MODEL), F32),
        compiler_params=_cparams(("arbitrary", "arbitrary")),
        name="mods",
    )(cvec, mod_w, mod_b.reshape(depth, 1, N_MOD * D_MODEL))


def _inproj_kernel(x_ref, g_ref, sc_ref, sh_ref, w_ref, oa_ref, ob_ref, oc_ref):
    x = x_ref[0]
    xn = x * lax.rsqrt(jnp.mean(x * x, axis=-1, keepdims=True) + EPS) * g_ref[...]
    xm = xn * (1.0 + sc_ref[0, 0, 0]) + sh_ref[0, 0, 0]
    p = _dot(xm, w_ref[0])
    oa_ref[0] = p[:, :S5_WIDTH]
    ob_ref[0] = p[:, S5_WIDTH:S5_WIDTH + GDN_COLS_PACKED]
    oc_ref[0] = p[:, S5_WIDTH + GDN_COLS_PACKED:]


def _mod_spec(layer, fixed_row, k):
    def index(i, t):
        return (layer, i if fixed_row is None else fixed_row, k, 0, 0)

    return pl.BlockSpec((1, 1, 1, 1, D_MODEL), index)


def _inproj(x, gain, mods, fixed_row, w_packed, layer):
    b, l, _ = x.shape
    tm = min(ROW_TILE, l)
    cols = w_packed.shape[2]
    return pl.pallas_call(
        _inproj_kernel,
        grid=(b, l // tm),
        in_specs=[
            pl.BlockSpec((1, tm, D_MODEL), lambda i, t: (i, t, 0)),
            pl.BlockSpec((1, D_MODEL), lambda i, t: (0, 0)),
            _mod_spec(layer, fixed_row, 1), _mod_spec(layer, fixed_row, 0),
            pl.BlockSpec((1, D_MODEL, cols), lambda i, t: (layer, 0, 0),
                         pipeline_mode=pl.Buffered(1)),
        ],
        out_specs=[
            pl.BlockSpec((1, tm, S5_WIDTH), lambda i, t: (i, t, 0)),
            pl.BlockSpec((1, tm, GDN_COLS_PACKED), lambda i, t: (i, t, 0)),
            pl.BlockSpec((1, tm, RWKV_COLS), lambda i, t: (i, t, 0)),
        ],
        out_shape=[
            jax.ShapeDtypeStruct((b, l, S5_WIDTH), F32),
            jax.ShapeDtypeStruct((b, l, GDN_COLS_PACKED), F32),
            jax.ShapeDtypeStruct((b, l, RWKV_COLS), F32),
        ],
        compiler_params=_cparams(("parallel", "arbitrary")),
        name="inproj",
    )(x, gain.reshape(1, D_MODEL), mods, mods, w_packed)


def _outproj_kernel(x_ref, ya_ref, yb_ref, yc_ref, wo_ref, gt1_ref,
                    g_ref, sc_ref, sh_ref, gt2_ref, w1_ref, w2_ref, gf_ref, o_ref, *, final):
    o_b = S5_WIDTH
    o_c = S5_WIDTH + WIDTH
    mix = (_dot(ya_ref[0], wo_ref[0, :o_b]) + _dot(yb_ref[0], wo_ref[0, o_b:o_c])
           + _dot(yc_ref[0], wo_ref[0, o_c:]))
    x1 = x_ref[0] + gt1_ref[0, 0, 0] * mix
    h = x1 * lax.rsqrt(jnp.mean(x1 * x1, axis=-1, keepdims=True) + EPS) * g_ref[...]
    h = h * (1.0 + sc_ref[0, 0, 0]) + sh_ref[0, 0, 0]
    a = jnp.maximum(_dot(h, w1_ref[0]), 0.0)
    x2 = x1 + gt2_ref[0, 0, 0] * _dot(a * a, w2_ref[0])
    if final:
        x2 = x2 * lax.rsqrt(jnp.mean(x2 * x2, axis=-1, keepdims=True) + EPS) * gf_ref[...]
    o_ref[0] = x2


def _outproj(x, ya, yb, yc, wo, mods, fixed_row, gain, w1, w2, gain_final, final, layer):
    b, l, _ = x.shape
    tm = min(ROW_TILE, l)
    row = pl.BlockSpec((1, D_MODEL), lambda i, t: (0, 0))

    def mod(k):
        return _mod_spec(layer, fixed_row, k)

    def tok(width):
        return pl.BlockSpec((1, tm, width), lambda i, t: (i, t, 0))

    def stacked(arr):
        return pl.BlockSpec((1,) + arr.shape[1:], lambda i, t: (layer, 0, 0),
                            pipeline_mode=pl.Buffered(1))

    return pl.pallas_call(
        functools.partial(_outproj_kernel, final=final),
        grid=(b, l // tm),
        in_specs=[tok(D_MODEL), tok(S5_WIDTH), tok(WIDTH), tok(WIDTH),
                  stacked(wo), mod(2), row, mod(4), mod(3), mod(5),
                  stacked(w1), stacked(w2), row],
        out_specs=tok(D_MODEL),
        out_shape=jax.ShapeDtypeStruct((b, l, D_MODEL), F32),
        compiler_params=_cparams(("parallel", "arbitrary")),
        name="outproj_mlp",
    )(x, ya, yb, yc, wo, mods, gain.reshape(1, D_MODEL), mods, mods, mods, w1, w2,
      gain_final.reshape(1, D_MODEL))


def _cmul(a, b):
    return a[0] * b[0] - a[1] * b[1], a[0] * b[1] + a[1] * b[0]


def _s5_kernel(*refs, rev, readout, n_chunks):
    (u_ref, are_ref, aim_ref, ldt_ref, bre_ref, bim_ref, cre_ref, cim_ref,
     h0re_ref, h0im_ref) = refs[:10]
    if readout:
        d_ref, gw_ref, gb_ref, yf_ref = refs[10:14]
        y_ref, hre_ref, him_ref = refs[14:17]
        wre, wim, abt, pw, carry, perm_ref = refs[17:]
    else:
        y_ref, hre_ref, him_ref = refs[10:13]
        wre, wim, abt, pw, carry, perm_ref = refs[13:]
    t = u_ref.shape[1]
    seg = t // SUBLANES
    j = pl.program_id(1)
    tile = (SUBLANES, S5_N)

    @pl.when(j == 0)
    def _seed_carry():
        carry[0:1, :] = h0re_ref[0]
        carry[1:2, :] = h0im_ref[0]

    @pl.when((pl.program_id(0) == 0) & (j == 0))
    def _discretise():
        dt = jnp.exp(ldt_ref[...])
        a_re = are_ref[...]
        a_im = aim_ref[...]
        mag = jnp.exp(a_re * dt)
        ab = (mag * jnp.cos(a_im * dt), mag * jnp.sin(a_im * dt))
        den = a_re * a_re + a_im * a_im
        xr = ab[0] - 1.0
        cr = (xr * a_re + ab[1] * a_im) / den
        ci = (ab[1] * a_re - xr * a_im) / den
        wre[...] = (bre_ref[...] * cr - bim_ref[...] * ci).astype(BF16)
        wim[...] = (bre_ref[...] * ci + bim_ref[...] * cr).astype(BF16)
        abt[0] = jnp.broadcast_to(ab[0], tile)
        abt[1] = jnp.broadcast_to(ab[1], tile)
        power = ab
        for k in range(seg):
            pw[0, k] = jnp.broadcast_to(power[0], tile)
            pw[1, k] = jnp.broadcast_to(power[1], tile)
            if k + 1 < seg:
                power = _cmul(power, ab)
        m_idx = lax.broadcasted_iota(jnp.int32, (t, t), 0)
        n_idx = lax.broadcasted_iota(jnp.int32, (t, t), 1)
        src = seg * (m_idx & (SUBLANES - 1)) + (m_idx >> int(math.log2(SUBLANES)))
        perm_ref[...] = jnp.where(n_idx == src, 1.0, 0.0).astype(BF16)

    u = u_ref[0]
    perm = perm_ref[...]
    u_p = _dot(perm, u)
    bur = _dot(u_p, wre[...])
    bui = _dot(u_p, wim[...])
    steps = range(seg - 1, -1, -1) if rev else range(seg)
    a_r = abt[0]
    a_i = abt[1]
    hr = jnp.zeros(tile, F32)
    hi = jnp.zeros(tile, F32)
    local = [None] * seg
    for k in steps:
        rows = slice(k * SUBLANES, (k + 1) * SUBLANES)
        hr, hi = a_r * hr - a_i * hi + bur[rows], a_r * hi + a_i * hr + bui[rows]
        local[k] = (hr, hi)
    a_seg = (pw[0, seg - 1][0:1], pw[1, seg - 1][0:1])
    c = (carry[0:1, :], carry[1:2, :])
    row = lax.broadcasted_iota(jnp.int32, tile, 0)
    c_re = jnp.zeros(tile, F32)
    c_im = jnp.zeros(tile, F32)
    for r in (range(SUBLANES - 1, -1, -1) if rev else range(SUBLANES)):
        c_re = jnp.where(row == r, c[0], c_re)
        c_im = jnp.where(row == r, c[1], c_im)
        c = (hr[r:r + 1] + a_seg[0] * c[0] - a_seg[1] * c[1],
             hi[r:r + 1] + a_seg[0] * c[1] + a_seg[1] * c[0])
    carry[0:1, :] = c[0]
    carry[1:2, :] = c[1]
    h_re = [None] * seg
    h_im = [None] * seg
    for n, k in enumerate(steps):
        h_re[k] = local[k][0] + pw[0, n] * c_re - pw[1, n] * c_im
        h_im[k] = local[k][1] + pw[0, n] * c_im + pw[1, n] * c_re
    y_p = (_dot(jnp.concatenate(h_re, axis=0), cre_ref[...])
           - _dot(jnp.concatenate(h_im, axis=0), cim_ref[...]))
    y_hi = y_p.astype(BF16)
    y = _dot(perm, y_hi, _TN) + _dot(perm, y_p - y_hi.astype(F32), _TN)
    c_re, c_im = c
    if readout:
        y = y + yf_ref[0] + d_ref[...] * u
        z = 0.5 * y * (1.0 + jnp.tanh(math.sqrt(2.0 / math.pi) * (y + 0.044715 * y * y * y)))
        y = z * _sigmoid(_dot(z, gw_ref[...]) + gb_ref[...])
    y_ref[0] = y

    @pl.when(j == n_chunks - 1)
    def _emit_state():
        hre_ref[0] = c_re
        him_ref[0] = c_im


def _s5_call(u, prm, d, h0, rev, readout, yf=None):
    b, l, _ = u.shape
    t = min(S5_BLOCK, l)
    n_chunks = l // t
    tok = pl.BlockSpec((1, t, S5_WIDTH), lambda i, j: (i, _chunk_index(j, n_chunks, rev), 0))
    row_n = pl.BlockSpec((1, S5_N), lambda i, j: (0, 0))
    st = pl.BlockSpec((1, 1, S5_N), lambda i, j: (i, 0, 0))

    def full(arr):
        return pl.BlockSpec(arr.shape, lambda i, j: (0,) * arr.ndim)

    args = [u, prm["a_re"][d], prm["a_im"][d], prm["log_dt"][d], prm["b_re"], prm["b_im"],
            prm["c_re"], prm["c_im"], h0[0], h0[1]]
    specs = [tok, row_n, row_n, row_n, full(prm["b_re"]), full(prm["b_im"]),
             full(prm["c_re"]), full(prm["c_im"]), st, st]
    if readout:
        args += [prm["d"], prm["glu_w"], prm["glu_b"], yf]
        specs += [full(prm["d"]), full(prm["glu_w"]), full(prm["glu_b"]), tok]
    y, hre, him = pl.pallas_call(
        functools.partial(_s5_kernel, rev=rev, readout=readout, n_chunks=n_chunks),
        grid=(b, n_chunks),
        in_specs=specs,
        out_specs=[tok, st, st],
        out_shape=[jax.ShapeDtypeStruct((b, l, S5_WIDTH), F32),
                   jax.ShapeDtypeStruct((b, 1, S5_N), F32),
                   jax.ShapeDtypeStruct((b, 1, S5_N), F32)],
        scratch_shapes=[pltpu.VMEM((S5_WIDTH, S5_N), BF16), pltpu.VMEM((S5_WIDTH, S5_N), BF16),
                        pltpu.VMEM((2, SUBLANES, S5_N), F32),
                        pltpu.VMEM((2, t // SUBLANES, SUBLANES, S5_N), F32),
                        pltpu.VMEM((2, S5_N), F32), pltpu.VMEM((t, t), BF16)],
        compiler_params=_cparams(("arbitrary", "arbitrary")),
        name="s5_bwd" if rev else "s5_fwd",
    )(*args)
    return y, (hre, him)


def _s5_mixer(u_ctx, u_lat, prm):
    b = u_lat.shape[0]
    zero = (jnp.zeros((b, 1, S5_N), F32), jnp.zeros((b, 1, S5_N), F32))
    yc_f, hc_f = _s5_call(u_ctx, prm, 0, zero, False, False)
    y_ctx, hc_b = _s5_call(u_ctx, prm, 1, zero, True, True, yc_f)
    yl_f, _ = _s5_call(u_lat, prm, 0, hc_f, False, False)
    y_lat, _ = _s5_call(u_lat, prm, 1, hc_b, True, True, yl_f)
    return y_ctx, y_lat


def _block_order(nc, rev):
    return range(nc - 1, -1, -1) if rev else range(nc)


def _store_heads(buf, ci, o):
    for h in range(HEADS):
        buf[ci * CHUNK:(ci + 1) * CHUNK, h * HEAD:(h + 1) * HEAD] = o[h]


def _gdn_kernel(*refs, rev, readout, n_blocks, d):
    pp_ref, pc_ref, pn_ref, cw_ref, alog_ref, dtb_ref, ones_ref, s0_ref = refs[:8]
    if readout:
        nw_ref, of_ref = refs[8:10]
        y_ref, so_ref = refs[10:12]
        state, obuf = refs[12:]
    else:
        y_ref, so_ref = refs[8:10]
        state, obuf = refs[10:]
    bt = pc_ref.shape[1]
    j = pl.program_id(1)
    c = _chunk_index(j, n_blocks, rev)

    @pl.when(j == 0)
    def _load_state():
        state[...] = s0_ref[0]

    halo = pp_ref.shape[1]
    ext = jnp.concatenate([_halo_rows(pp_ref, c > 0), pc_ref[0],
                           _halo_rows(pn_ref, c < n_blocks - 1)], axis=0)
    incl = _tri_mask(rev, False)
    strict = _tri_mask(rev, True)
    sub = min(SUB_CHUNKS * CHUNK, bt)
    nc = sub // CHUNK
    cum = _chunk_cumsum_matrix(sub, rev)
    half = GDN_CONV // 2
    first_g = 2 * HEADS + d * HEADS
    edge = 0 if rev else CHUNK - 1

    def prepare(si):
        lo = si * sub
        seg = ext[lo:lo + sub + 2 * halo]
        x = seg[halo:halo + sub]
        qkv = seg[:, :3 * WIDTH]
        acc = cw_ref[half:half + 1, :] * x[:, :3 * WIDTH]
        for off in range(-half, half + 1):
            if off != 0:
                tap = pltpu.roll(qkv, (-off) % (sub + 2 * halo), 0)[halo:halo + sub]
                acc = acc + cw_ref[half + off:half + off + 1, :] * tap
                yield
        act = _silu(acc)
        q = act[:, :WIDTH]
        k = act[:, WIDTH:2 * WIDTH]
        v = act[:, 2 * WIDTH:]
        q2 = q * q
        k2 = k * k
        tail = x[:, 4 * WIDTH:]
        beta_all = _sigmoid(tail)
        g_all = -jnp.exp(alog_ref[...]) * _softplus(tail + dtb_ref[...])
        gated = nw_ref[...] * _silu(x[:, 3 * WIDTH:4 * WIDTH]) if readout else None
        yield
        yield
        q_ssq = _head_sum(q2, ones_ref)
        k_ssq = _head_sum(k2, ones_ref)
        gcs = _dot_select(cum, g_all)
        yield
        yield
        q = q * (lax.rsqrt(q_ssq + EPS) * HEAD ** -0.5)
        k = k * lax.rsqrt(k_ssq + EPS)
        gcs_t = gcs.T
        gc = _split_cols(gcs, first_g)
        gr = jnp.stack([gcs_t[first_g + h:first_g + h + 1, ci * CHUNK:(ci + 1) * CHUNK]
                        for ci in range(nc) for h in range(HEADS)], axis=0)
        beta = _split_cols(beta_all, d * HEADS)
        yield
        decay = jnp.where(incl, jnp.exp(jnp.minimum(gc - gr, 0.0)), 0.0)
        yield
        qs = _split_heads(q)
        ks = _split_heads(k)
        yield
        vs = _split_heads(v)
        kb = ks * beta
        eg = jnp.exp(gc)
        g_last = gc[:, edge:edge + 1]
        yield
        return dict(lhs=jnp.concatenate([kb, qs], axis=1), ks=ks, decay=decay,
                    vw=jnp.concatenate([vs * beta, kb * eg], axis=2), q_eg=qs * eg,
                    kd=ks * jnp.exp(g_last - gc), dec_last=jnp.exp(g_last), gated=gated)

    def chain(si, op, s):
        rows = slice(si * sub, (si + 1) * sub)
        big = _bmm(op["lhs"], op["ks"], tb=True)
        yield
        m = jnp.where(strict, big[:, :CHUNK] * op["decay"], 0.0)
        attn = big[:, CHUNK:] * op["decay"]
        yield
        tinv = yield from _inv_unit_tri(m)
        uw = _bmm(tinv, op["vw"])
        yield
        au = _bmm(attn, uw)
        yield
        q_eff = op["q_eg"] - au[:, :, HEAD:]
        y_in = au[:, :, :HEAD]
        kdu = _bmm(op["kd"], uw, ta=True)
        yield
        first = si * nc
        for ci in _block_order(nc, rev):
            sl = slice(ci * HEADS, (ci + 1) * HEADS)
            _store_heads(obuf, first + ci, _bmm(q_eff[sl], s) + y_in[sl])
            s = s * op["dec_last"][sl] - _bmm(kdu[sl, :, HEAD:], s) + kdu[sl, :, :HEAD]
            yield
        o = obuf[rows]
        if readout:
            o = o + of_ref[0, rows]
            scale = lax.rsqrt(_head_sum(o * o, ones_ref) * (1.0 / HEAD) + EPS)
            o = o * scale * op["gated"]
        y_ref[0, rows] = o
        return s

    order = list(_block_order(bt // sub, rev))
    s = state[...]
    op = _run(prepare(order[0]))
    for n, si in enumerate(order):
        if n + 1 < len(order):
            s, op = _interleave(chain(si, op, s), prepare(order[n + 1]), CHAIN_LEAD)
        else:
            s = _run(chain(si, op, s))
    state[...] = s

    @pl.when(j == n_blocks - 1)
    def _emit_state():
        so_ref[0] = state[...]


def _block_rows(l, chunks):
    return min(chunks * CHUNK, l)


def _halo_specs(width, bt, n_blocks, rev, halo):
    per_block = bt // halo
    last = n_blocks * per_block - 1

    def cur(i, j):
        return (i, _chunk_index(j, n_blocks, rev), 0)

    def prev(i, j):
        return (i, jnp.maximum(_chunk_index(j, n_blocks, rev) * per_block - 1, 0), 0)

    def nxt(i, j):
        return (i, jnp.minimum((_chunk_index(j, n_blocks, rev) + 1) * per_block, last), 0)

    return [pl.BlockSpec((1, halo, width), prev), pl.BlockSpec((1, bt, width), cur),
            pl.BlockSpec((1, halo, width), nxt)]


def _gdn_call(p, prm, d, s0, rev, readout, of=None):
    b, l, _ = p.shape
    bt = _block_rows(l, BLOCK_CHUNKS)
    n_blocks = l // bt
    tok = pl.BlockSpec((1, bt, WIDTH), lambda i, j: (i, _chunk_index(j, n_blocks, rev), 0))
    st = pl.BlockSpec((1, HEADS, HEAD, HEAD), lambda i, j: (i, 0, 0, 0))

    def full(arr):
        return pl.BlockSpec(arr.shape, lambda i, j: (0,) * arr.ndim)

    args = [p, p, p, prm["conv"], prm["alog"][d], prm["dtb"][d], prm["ones"], s0]
    specs = _halo_specs(GDN_COLS_PACKED, bt, n_blocks, rev, SUBLANES) + [
        full(prm["conv"]), full(prm["alog"][d]), full(prm["dtb"][d]), full(prm["ones"]), st]
    if readout:
        args += [prm["norm"], of]
        specs += [full(prm["norm"]), tok]
    return pl.pallas_call(
        functools.partial(_gdn_kernel, rev=rev, readout=readout, n_blocks=n_blocks, d=d),
        grid=(b, n_blocks),
        in_specs=specs,
        out_specs=[tok, st],
        out_shape=[jax.ShapeDtypeStruct((b, l, WIDTH), F32),
                   jax.ShapeDtypeStruct((b, HEADS, HEAD, HEAD), F32)],
        scratch_shapes=[pltpu.VMEM((HEADS, HEAD, HEAD), F32), pltpu.VMEM((bt, WIDTH), F32)],
        compiler_params=_cparams(("parallel", "arbitrary")),
        name="gdn_bwd" if rev else "gdn_fwd",
    )(*args)


def _gdn_mixer(p_ctx, p_lat, prm):
    b = p_lat.shape[0]
    zero = jnp.zeros((b, HEADS, HEAD, HEAD), F32)
    oc_f, sc_f = _gdn_call(p_ctx, prm, 0, zero, False, False)
    y_ctx, sc_b = _gdn_call(p_ctx, prm, 1, zero, True, True, oc_f)
    ol_f, _ = _gdn_call(p_lat, prm, 0, sc_f, False, False)
    y_lat, _ = _gdn_call(p_lat, prm, 1, sc_b, True, True, ol_f)
    return y_ctx, y_lat


def _rwkv_kernel(*refs, rev, readout, grid_shift, n_blocks):
    (pp_ref, pc_ref, pn_ref, mu_ref, w0_ref, wup_ref, a0_ref, aup_ref, kk_ref, ka_ref, rk_ref,
     ones_ref, s0_ref) = refs[:13]
    if readout:
        gup_ref, lnw_ref, lnb_ref, yf_ref, bf_ref = refs[13:18]
        y_ref, so_ref = refs[18:20]
        state, obuf = refs[20:]
    else:
        y_ref, bon_ref, so_ref = refs[13:16]
        state, obuf = refs[16:]
    bt = pc_ref.shape[1]
    j = pl.program_id(1)
    c = _chunk_index(j, n_blocks, rev)

    @pl.when(j == 0)
    def _load_state():
        state[...] = s0_ref[0]

    halo = pp_ref.shape[1]
    ext = jnp.concatenate([_halo_rows(pp_ref, c > 0), pc_ref[0],
                           _halo_rows(pn_ref, c < n_blocks - 1)], axis=0)
    incl = _tri_mask(rev, False)
    strict = _tri_mask(rev, True)
    sub = min(SUB_CHUNKS * CHUNK, bt)
    cum = _chunk_cumsum_matrix(sub, rev)

    def prepare(si):
        lo = si * sub
        seg = ext[lo:lo + sub + 2 * halo]
        x = seg[halo:halo + sub]
        lane = lax.broadcasted_iota(jnp.int32, x.shape, 1)
        if grid_shift:
            col = lax.broadcasted_iota(jnp.int32, x.shape, 0) & (GRID_W - 1)
            left = jnp.where(col == 0, 0.0, pltpu.roll(x, 1, 0))
            right = jnp.where(col == GRID_W - 1, 0.0, pltpu.roll(x, sub - 1, 0))
            cm = lane & 3
            shifted = jnp.where(cm == 0, left, jnp.where(
                cm == 1, right, jnp.where(cm == 2, seg[:sub], seg[2 * halo:])))
        else:
            shifted = jnp.where((lane & 1) == 0, pltpu.roll(seg, 1, 0)[halo:halo + sub],
                                pltpu.roll(seg, sub + 2 * halo - 1, 0)[halo:halo + sub])
        yield
        pm = x + (shifted - x) * mu_ref[...]
        r = pm[:, :WIDTH]
        k = pm[:, WIDTH:2 * WIDTH]
        v = pm[:, 2 * WIDTH:3 * WIDTH]
        o1 = 3 * WIDTH
        w_dn = pm[:, o1:o1 + 2 * LORA]
        a_dn = pm[:, o1 + 2 * LORA:o1 + 4 * LORA]
        g_dn = pm[:, o1 + 4 * LORA:]
        w_act = jnp.tanh(w_dn)
        g_act = _sigmoid(g_dn)
        kkr = k * kk_ref[...]
        kkr2 = kkr * kkr
        yield
        yield
        w_lora = _dot(w_act, wup_ref[...])
        a_lora = _dot(a_dn, aup_ref[...])
        kk_ssq = _head_sum(kkr2, ones_ref)
        gate = _dot(g_act, gup_ref[...]) if readout else None
        yield
        yield
        w_log = -_softplus(-(w0_ref[...] + w_lora)) - 0.5
        lw = -jnp.exp(w_log)
        iclr = _sigmoid(a0_ref[...] + a_lora)
        kk = kkr * lax.rsqrt(kk_ssq + EPS)
        k_dir = k * (1.0 + (iclr - 1.0) * ka_ref[...])
        b_dir = kk * iclr
        rkr = r * k_dir * rk_ref[...]
        yield
        yield
        g_inc = _dot_select(cum, lw)
        bonus = _head_sum(rkr, ones_ref) * v
        yield
        yield
        g_end = _chunk_edge_rows(g_inc, rev)
        e_neg = jnp.exp(-g_inc)
        e_end = jnp.exp(g_end - g_inc)
        yield
        lhs = jnp.concatenate([_split_heads(kk * jnp.exp(g_inc - lw)),
                               _split_heads(r * jnp.exp(g_inc))], axis=1)
        yield
        rhs = jnp.concatenate([_split_heads(b_dir * e_neg), _split_heads(k_dir * e_neg)], axis=1)
        yield
        b_eh = _split_heads(b_dir * e_end)
        yield
        k_eh = _split_heads(k_dir * e_end)
        yield
        v_h = _split_heads(v)
        dec_end = _split_heads(jnp.exp(g_end))[:, :1, :]
        return dict(lhs=lhs, rhs=rhs, b_eh=b_eh, k_eh=k_eh, v_h=v_h, dec_end=dec_end,
                    bonus=bonus, gate=gate)

    def chain(si, op, s):
        rows = slice(si * sub, (si + 1) * sub)
        kap_h = op["lhs"][:, :CHUNK]
        r_h = op["lhs"][:, CHUNK:]
        v_h = op["v_h"]
        big = _bmm(op["lhs"], op["rhs"], tb=True)
        yield
        a_b = jnp.where(strict, big[:, :CHUNK, :CHUNK], 0.0)
        a_k = jnp.where(strict, big[:, :CHUNK, CHUNK:], 0.0)
        r_b = jnp.where(incl, big[:, CHUNK:, :CHUNK], 0.0)
        r_k = jnp.where(incl, big[:, CHUNK:, CHUNK:], 0.0)
        akv = _bmm(a_k, v_h)
        yield
        tinv = yield from _inv_unit_tri(a_b)
        z = _bmm(tinv, jnp.concatenate([kap_h, akv], axis=2))
        yield
        rz = _bmm(r_b, z)
        yield
        q_eff = r_h - rz[:, :, :HEAD]
        y_in = _bmm(r_k, v_h) - rz[:, :, HEAD:]
        yield
        ztb = _bmm(z, op["b_eh"], ta=True)
        yield
        pt = ztb[:, :HEAD]
        xt = _bmm(v_h, op["k_eh"], ta=True) - ztb[:, HEAD:]
        yield
        first = si * (sub // CHUNK)
        for ci in _block_order(sub // CHUNK, rev):
            sl = slice(ci * HEADS, (ci + 1) * HEADS)
            _store_heads(obuf, first + ci, _bmm(q_eff[sl], s, tb=True) + y_in[sl])
            s = s * op["dec_end"][sl] - _bmm(s, pt[sl]) + xt[sl]
            yield
        y = obuf[rows]
        if readout:
            y = y + yf_ref[0, rows]
            yc = y - _head_sum(y, ones_ref) * (1.0 / HEAD)
            var = _head_sum(yc * yc, ones_ref) * (1.0 / HEAD)
            yn = yc * lax.rsqrt(var + RWKV_GN_EPS) * lnw_ref[...] + lnb_ref[...]
            y_ref[0, rows] = (yn + op["bonus"] + bf_ref[0, rows]) * op["gate"]
        else:
            y_ref[0, rows] = y
            bon_ref[0, rows] = op["bonus"]
        return s

    order = list(_block_order(bt // sub, rev))
    s = state[...]
    op = _run(prepare(order[0]))
    for n, si in enumerate(order):
        if n + 1 < len(order):
            s, op = _interleave(chain(si, op, s), prepare(order[n + 1]), CHAIN_LEAD)
        else:
            s = _run(chain(si, op, s))
    state[...] = s

    @pl.when(j == n_blocks - 1)
    def _emit_state():
        so_ref[0] = state[...]


def _rwkv_call(p, prm, d, s0, rev, readout, grid_shift, yf=None, bf=None):
    b, l, _ = p.shape
    bt = _block_rows(l, BLOCK_CHUNKS)
    n_blocks = l // bt
    tok = pl.BlockSpec((1, bt, WIDTH), lambda i, j: (i, _chunk_index(j, n_blocks, rev), 0))
    st = pl.BlockSpec((1, HEADS, HEAD, HEAD), lambda i, j: (i, 0, 0, 0))

    def full(arr):
        return pl.BlockSpec(arr.shape, lambda i, j: (0,) * arr.ndim)

    small = [prm["mu"], prm["w0"][d], prm["w_up"][d], prm["a0"][d], prm["a_up"][d],
             prm["k_k"], prm["k_a"], prm["r_k"], prm["ones"]]
    args = [p, p, p] + small + [s0]
    halo = GRID_W if grid_shift else SUBLANES
    specs = _halo_specs(RWKV_COLS, bt, n_blocks, rev, halo) + [full(a) for a in small] + [st]
    tok_shape = jax.ShapeDtypeStruct((b, l, WIDTH), F32)
    st_shape = jax.ShapeDtypeStruct((b, HEADS, HEAD, HEAD), F32)
    if readout:
        extra = [prm["g_up"], prm["ln_w"], prm["ln_b"]]
        args += extra + [yf, bf]
        specs += [full(a) for a in extra] + [tok, tok]
        out_specs, out_shape = [tok, st], [tok_shape, st_shape]
    else:
        out_specs, out_shape = [tok, tok, st], [tok_shape, tok_shape, st_shape]
    return pl.pallas_call(
        functools.partial(_rwkv_kernel, rev=rev, readout=readout, grid_shift=grid_shift,
                          n_blocks=n_blocks),
        grid=(b, n_blocks),
        in_specs=specs,
        out_specs=out_specs,
        out_shape=out_shape,
        scratch_shapes=[pltpu.VMEM((HEADS, HEAD, HEAD), F32), pltpu.VMEM((bt, WIDTH), F32)],
        compiler_params=_cparams(("parallel", "arbitrary")),
        name="rwkv_bwd" if rev else "rwkv_fwd",
    )(*args)


def _rwkv_mixer(p_ctx, p_lat, prm):
    b = p_lat.shape[0]
    zero = jnp.zeros((b, HEADS, HEAD, HEAD), F32)
    yc_f, bc_f, sc_f = _rwkv_call(p_ctx, prm, 0, zero, False, False, False)
    y_ctx, sc_b = _rwkv_call(p_ctx, prm, 1, zero, True, True, False, yc_f, bc_f)
    yl_f, bl_f, _ = _rwkv_call(p_lat, prm, 0, sc_f, False, False, True)
    y_lat, _ = _rwkv_call(p_lat, prm, 1, sc_b, True, True, True, yl_f, bl_f)
    return y_ctx, y_lat


def _pack_w_in(w):
    o_rwkv = S5_WIDTH + 4 * WIDTH + 4 * HEADS
    pad = jnp.zeros(w.shape[:-1] + (GDN_TAIL - 4 * HEADS,), w.dtype)
    return jnp.concatenate([w[..., :o_rwkv], pad, w[..., o_rwkv:]], axis=-1)


def _block_diag(blocks):
    g, r, c = blocks.shape
    eye = jnp.eye(g, dtype=blocks.dtype)
    return (eye[:, None, :, None] * blocks[:, :, None, :]).reshape(g * r, g * c)


def _s5_params(layer, b_re, b_im, c_re, c_im, d, a_re, a_im, log_dt, glu_w, glu_b):
    row = lambda v: v.reshape(2, 1, S5_N)
    return {
        "a_re": row(a_re[layer]), "a_im": row(a_im[layer]),
        "log_dt": row(jnp.repeat(log_dt[layer], S5_STATE, axis=-1)),
        "b_re": _block_diag(jnp.swapaxes(b_re[layer], 1, 2)),
        "b_im": _block_diag(jnp.swapaxes(b_im[layer], 1, 2)),
        "c_re": _block_diag(jnp.swapaxes(c_re[layer], 1, 2)),
        "c_im": _block_diag(jnp.swapaxes(c_im[layer], 1, 2)),
        "d": d[layer].reshape(1, S5_WIDTH), "glu_w": glu_w[layer],
        "glu_b": glu_b[layer].reshape(1, S5_WIDTH),
    }


def _head_ones():
    return _block_diag(jnp.ones((HEADS, HEAD, HEAD), BF16))


def _gdn_params(layer, conv, a_log, dt_bias, norm):
    def tail_row(v):
        rows = []
        for d in range(2):
            z = jnp.zeros((GDN_TAIL,), F32)
            rows.append(z.at[2 * HEADS + d * HEADS:2 * HEADS + (d + 1) * HEADS].set(v[d])
                        .reshape(1, GDN_TAIL))
        return rows

    return {"conv": conv[layer], "alog": tail_row(a_log[layer]), "dtb": tail_row(dt_bias[layer]),
            "norm": jnp.tile(norm[layer], HEADS).reshape(1, WIDTH), "ones": _head_ones()}


def _rwkv_params(layer, mu, w0, w_up, a0, a_up, g_up, k_k, k_a, r_k, ln_w, ln_b):
    def lora_pad(w):
        z = jnp.zeros((LORA, WIDTH), F32)
        return [jnp.concatenate([w[0], z], axis=0), jnp.concatenate([z, w[1]], axis=0)]

    row = lambda v: v.reshape(1, -1)
    return {
        "mu": row(mu[layer]), "w0": [row(w0[layer, 0]), row(w0[layer, 1])],
        "w_up": lora_pad(w_up[layer]), "a0": [row(a0[layer, 0]), row(a0[layer, 1])],
        "a_up": lora_pad(a_up[layer]), "g_up": g_up[layer], "k_k": row(k_k[layer]),
        "k_a": row(k_a[layer]), "r_k": row(r_k[layer]), "ln_w": row(ln_w[layer]),
        "ln_b": row(ln_b[layer]), "ones": _head_ones(),
    }


def kernel(x, c, ctx, c_ctx, mod_w, mod_b, norm_mix, norm_mlp, norm_final, w_in, w_out, s5_b_re, s5_b_im, s5_c_re, s5_c_im, s5_d, s5_a_re, s5_a_im, s5_log_dt, s5_glu_w, s5_glu_b, gdn_conv, gdn_a_log, gdn_dt_bias, gdn_norm, rwkv_mu, rwkv_w0, rwkv_w_up, rwkv_a0, rwkv_a_up, rwkv_g_up, rwkv_k_k, rwkv_k_a, rwkv_r_k, rwkv_ln_w, rwkv_ln_b, mlp_w1, mlp_w2):
    depth = mod_w.shape[0]
    batch = x.shape[0]
    x = x.astype(F32)
    ctx = ctx.astype(F32)
    rows = SUBLANES * ((batch + 1 + SUBLANES - 1) // SUBLANES)
    cvec = jnp.zeros((rows, D_MODEL), F32).at[:batch].set(c.astype(F32)).at[batch].set(
        c_ctx.astype(F32))
    mods = _mods(cvec, mod_w, mod_b).reshape(depth, rows, N_MOD, 1, D_MODEL)
    w_packed = _pack_w_in(w_in.astype(BF16))
    wo = w_out.astype(BF16)
    w1 = mlp_w1.astype(BF16)
    w2 = mlp_w2.astype(BF16)

    for layer in range(depth):
        ctx_out = layer < depth - 1
        pa_l, pb_l, pc_l = _inproj(x, norm_mix[layer], mods, None, w_packed, layer)
        pa_c, pb_c, pc_c = _inproj(ctx, norm_mix[layer], mods, batch, w_packed, layer)

        ya_c, ya_l = _s5_mixer(pa_c, pa_l, _s5_params(
            layer, s5_b_re, s5_b_im, s5_c_re, s5_c_im, s5_d, s5_a_re, s5_a_im, s5_log_dt,
            s5_glu_w, s5_glu_b))
        yb_c, yb_l = _gdn_mixer(pb_c, pb_l, _gdn_params(
            layer, gdn_conv, gdn_a_log, gdn_dt_bias, gdn_norm))
        yc_c, yc_l = _rwkv_mixer(pc_c, pc_l, _rwkv_params(
            layer, rwkv_mu, rwkv_w0, rwkv_w_up, rwkv_a0, rwkv_a_up, rwkv_g_up, rwkv_k_k,
            rwkv_k_a, rwkv_r_k, rwkv_ln_w, rwkv_ln_b))

        x = _outproj(x, ya_l, yb_l, yc_l, wo, mods, None, norm_mlp[layer], w1, w2, norm_final,
                     not ctx_out, layer)
        if ctx_out:
            ctx = _outproj(ctx, ya_c, yb_c, yc_c, wo, mods, batch, norm_mlp[layer], w1, w2,
                           norm_final, False, layer)
    return x
```
